```python
import jax, jax.numpy as jnp
from jax import lax
import numpy as np

D_MODEL = 1024
BATCH = 4
SEQ = 8192
DEPTH = 1
DEC_BATCH = 1
DEC_SEQ = 16384
PAST_LEN = 128

MIX_WIDTH = D_MODEL
ATT_HEADS = 8
NOPE_DIM = 64
ROPE_DIM = 32
QK_DIM = NOPE_DIM + ROPE_DIM
V_DIM = 64
Q_RANK = 384
KV_RANK = 256
ATT_WIDTH = ATT_HEADS * V_DIM
ROPE_THETA = 10000.0
Q_BLOCK = 128
RWKV_HEADS = 8
RWKV_HEAD_DIM = 64
RWKV_WIDTH = RWKV_HEADS * RWKV_HEAD_DIM
DECAY_LORA = 64
ICLR_LORA = 64
GATE_LORA = 128
D_FF = 2816
NORM_EPS = 1e-6
LN_X_EPS = 64e-5

OFF_CKV = Q_RANK
OFF_KR = OFF_CKV + KV_RANK
OFF_RWKV = OFF_KR + ROPE_DIM
RWKV_COLS = 3 * RWKV_WIDTH + 2 * DECAY_LORA + 2 * ICLR_LORA + GATE_LORA
IN_COLS = OFF_RWKV + RWKV_COLS
RWKV_SPLITS = (RWKV_WIDTH, 2 * RWKV_WIDTH, 3 * RWKV_WIDTH,
               3 * RWKV_WIDTH + DECAY_LORA, 3 * RWKV_WIDTH + 2 * DECAY_LORA,
               3 * RWKV_WIDTH + 2 * DECAY_LORA + ICLR_LORA,
               3 * RWKV_WIDTH + 2 * DECAY_LORA + 2 * ICLR_LORA)

kernel_name = "hymba_mla_rwkv7_macaron_encoder"


def rmsnorm(x, g):
    xf = x.astype(jnp.float32)
    y = xf * lax.rsqrt(jnp.mean(xf * xf, axis=-1, keepdims=True) + NORM_EPS)
    return (y * g.astype(jnp.float32)).astype(x.dtype)


def swiglu(x, w_gate, w_up, w_down):
    return (jax.nn.silu(x @ w_gate) * (x @ w_up)) @ w_down


def rope_tables(seq_len):
    inv = 1.0 / (ROPE_THETA ** (jnp.arange(0, ROPE_DIM, 2, dtype=jnp.float32) / ROPE_DIM))
    ang = jnp.arange(seq_len, dtype=jnp.float32)[:, None] * inv[None, :]
    return jnp.cos(ang)[:, None, :], jnp.sin(ang)[:, None, :]


def apply_rope(x, cos, sin):
    xf = x.astype(jnp.float32)
    half = ROPE_DIM // 2
    x1, x2 = xf[..., :half], xf[..., half:]
    return jnp.concatenate([x1 * cos - x2 * sin, x1 * sin + x2 * cos], axis=-1).astype(x.dtype)


def mla_group(z_cq, z_ckv, z_kr, q_norm, w_uq, kv_norm, w_ukv, attn_out_norm):
    B, S, _ = z_cq.shape
    q = (rmsnorm(z_cq, q_norm) @ w_uq).reshape(B, S, ATT_HEADS, QK_DIM)
    kv = (rmsnorm(z_ckv, kv_norm) @ w_ukv).reshape(B, S, ATT_HEADS, NOPE_DIM + V_DIM)
    k_nope, v = kv[..., :NOPE_DIM], kv[..., NOPE_DIM:]
    cos, sin = rope_tables(S)
    q = jnp.concatenate([q[..., :NOPE_DIM], apply_rope(q[..., NOPE_DIM:], cos, sin)], axis=-1)
    k_pe = apply_rope(z_kr[:, :, None, :], cos, sin)
    k = jnp.concatenate([k_nope, jnp.broadcast_to(k_pe, (B, S, ATT_HEADS, ROPE_DIM))], axis=-1)
    q = q * (QK_DIM ** -0.5)
    q_blocks = q.reshape(B, S // Q_BLOCK, Q_BLOCK, ATT_HEADS, QK_DIM).transpose(1, 0, 2, 3, 4)

    def attend(qb):
        s = jnp.einsum('bqhd,bkhd->bhqk', qb, k, preferred_element_type=jnp.float32)
        p = jax.nn.softmax(s, axis=-1)
        return jnp.einsum('bhqk,bkhd->bqhd', p.astype(v.dtype), v)

    o = lax.map(attend, q_blocks)
    o = o.transpose(1, 0, 2, 3, 4).reshape(B, S, ATT_WIDTH)
    return rmsnorm(o, attn_out_norm)


def centred_shift(z, mu):
    z_prev = jnp.pad(z[:, :-1], ((0, 0), (1, 0), (0, 0)))
    z_next = jnp.pad(z[:, 1:], ((0, 0), (0, 1), (0, 0)))
    return z + mu[0] * (z_prev - z) + mu[1] * (z_next - z)


def wkv_scan(r, w, k, v, kk, b, reverse):
    B, S, H, N = r.shape
    seq = tuple(t.astype(jnp.float32).transpose(1, 0, 2, 3) for t in (r, w, k, v, kk, b))

    def step(state, inp):
        r_t, w_t, k_t, v_t, kk_t, b_t = inp
        sa = jnp.einsum('bhvk,bhk->bhv', state, kk_t)
        state = (state * w_t[:, :, None, :] - sa[..., None] * b_t[:, :, None, :]
                 + v_t[..., None] * k_t[:, :, None, :])
        return state, jnp.einsum('bhvk,bhk->bhv', state, r_t)

    s0 = jnp.zeros((B, H, N, N), jnp.float32)
    _, out = lax.scan(step, s0, seq, reverse=reverse)
    return out.transpose(1, 0, 2, 3)


def rwkv7_group(z, shift_mu, w0, w_up, a0, a_up, g_up, k_k, k_a, r_k, ln_w, ln_b):
    B, S, _ = z.shape
    hs = lambda t: t.reshape(B, S, RWKV_HEADS, RWKV_HEAD_DIM)
    z = centred_shift(z, shift_mu)
    r, k, v, wd_f, wd_b, ad_f, ad_b, gd = jnp.split(z, RWKV_SPLITS, axis=-1)
    g = jax.nn.sigmoid(gd) @ g_up
    kkf = hs(k * k_k).astype(jnp.float32)
    kk = kkf / jnp.maximum(jnp.sqrt(jnp.sum(kkf * kkf, axis=-1, keepdims=True)), 1e-12)
    y = jnp.zeros((B, S, RWKV_HEADS, RWKV_HEAD_DIM), jnp.float32)
    k_sum = jnp.zeros_like(k)
    for d, (wd, ad, rev) in enumerate(((wd_f, ad_f, False), (wd_b, ad_b, True))):
        w_log = -jax.nn.softplus(-(w0[d] + jnp.tanh(wd) @ w_up[d])) - 0.5
        decay = jnp.exp(-jnp.exp(w_log.astype(jnp.float32)))
        a = jax.nn.sigmoid(a0[d] + ad @ a_up[d])
        k_d = k * (1.0 + (a - 1.0) * k_a)
        y = y + wkv_scan(hs(r), hs(decay), hs(k_d), hs(v), kk, kk * hs(a).astype(jnp.float32), rev)
        k_sum = k_sum + k_d
    mean = jnp.mean(y, axis=-1, keepdims=True)
    var = jnp.mean(jnp.square(y - mean), axis=-1, keepdims=True)
    y = ((y - mean) * lax.rsqrt(var + LN_X_EPS)).reshape(B, S, RWKV_WIDTH)
    y = y * ln_w.astype(jnp.float32) + ln_b.astype(jnp.float32)
    bonus = jnp.sum((hs(r) * hs(k_sum) * r_k.reshape(RWKV_HEADS, RWKV_HEAD_DIM)).astype(jnp.float32),
                    axis=-1, keepdims=True) * hs(v).astype(jnp.float32)
    y = y + bonus.reshape(B, S, RWKV_WIDTH)
    return (y.astype(z.dtype)) * g


def encoder_layer(h, p):
    h = h + 0.5 * swiglu(rmsnorm(h, p['ffn1_norm']), p['ffn1_w_gate'], p['ffn1_w_up'], p['ffn1_w_down'])
    u = rmsnorm(h, p['mix_norm'])
    z = u @ p['w_in']
    att = mla_group(z[..., :OFF_CKV], z[..., OFF_CKV:OFF_KR], z[..., OFF_KR:OFF_RWKV],
                    p['q_norm'], p['w_uq'], p['kv_norm'], p['w_ukv'], p['attn_out_norm'])
    tm = rwkv7_group(z[..., OFF_RWKV:], p['shift_mu'], p['w0'], p['w_up'], p['a0'], p['a_up'],
                     p['g_up'], p['k_k'], p['k_a'], p['r_k'], p['ln_x_w'], p['ln_x_b'])
    h = h + jnp.concatenate([att, tm], axis=-1) @ p['w_out']
    h = h + 0.5 * swiglu(rmsnorm(h, p['ffn2_norm']), p['ffn2_w_gate'], p['ffn2_w_up'], p['ffn2_w_down'])
    return h


def setup_inputs(seed: int = 0) -> dict:
    key = jax.random.key(seed)
    ks = iter(jax.random.split(key, 40))
    nrm = lambda shape, s: jax.random.normal(next(ks), shape, jnp.float32) * s
    gain = lambda shape: 1.0 + nrm(shape, 0.05)
    L = DEPTH
    return {
        'x_prompt': nrm((BATCH, SEQ, D_MODEL), 1.0),
        'x_sample': nrm((DEC_BATCH, DEC_SEQ, D_MODEL), 1.0),
        'ffn1_norm': gain((L, D_MODEL)),
        'ffn1_w_gate': nrm((L, D_MODEL, D_FF), D_MODEL ** -0.5),
        'ffn1_w_up': nrm((L, D_MODEL, D_FF), D_MODEL ** -0.5),
        'ffn1_w_down': nrm((L, D_FF, D_MODEL), D_FF ** -0.5),
        'mix_norm': gain((L, D_MODEL)),
        'w_in': nrm((L, D_MODEL, IN_COLS), D_MODEL ** -0.5),
        'q_norm': gain((L, Q_RANK)),
        'w_uq': nrm((L, Q_RANK, ATT_HEADS * QK_DIM), Q_RANK ** -0.5),
        'kv_norm': gain((L, KV_RANK)),
        'w_ukv': nrm((L, KV_RANK, ATT_HEADS * (NOPE_DIM + V_DIM)), KV_RANK ** -0.5),
        'attn_out_norm': gain((L, ATT_WIDTH)),
        'shift_mu': jax.random.uniform(next(ks), (L, 2, RWKV_COLS), jnp.float32, 0.0, 0.5),
        'w0': nrm((L, 2, RWKV_WIDTH), 0.5),
        'w_up': nrm((L, 2, DECAY_LORA, RWKV_WIDTH), 0.1),
        'a0': nrm((L, 2, RWKV_WIDTH), 0.5),
        'a_up': nrm((L, 2, ICLR_LORA, RWKV_WIDTH), 0.1),
        'g_up': nrm((L, GATE_LORA, RWKV_WIDTH), GATE_LORA ** -0.5),
        'k_k': 0.85 + nrm((L, RWKV_WIDTH), 0.05),
        'k_a': gain((L, RWKV_WIDTH)),
        'r_k': nrm((L, RWKV_WIDTH), 0.1),
        'ln_x_w': gain((L, RWKV_WIDTH)),
        'ln_x_b': nrm((L, RWKV_WIDTH), 0.01),
        'w_out': nrm((L, MIX_WIDTH, D_MODEL), MIX_WIDTH ** -0.5),
        'ffn2_norm': gain((L, D_MODEL)),
        'ffn2_w_gate': nrm((L, D_MODEL, D_FF), D_MODEL ** -0.5),
        'ffn2_w_up': nrm((L, D_MODEL, D_FF), D_MODEL ** -0.5),
        'ffn2_w_down': nrm((L, D_FF, D_MODEL), D_FF ** -0.5),
        'final_norm': gain((D_MODEL,)),
    }


def reference(x_prompt, x_sample, ffn1_norm, ffn1_w_gate, ffn1_w_up, ffn1_w_down, mix_norm, w_in,
              q_norm, w_uq, kv_norm, w_ukv, attn_out_norm, shift_mu, w0, w_up, a0, a_up, g_up,
              k_k, k_a, r_k, ln_x_w, ln_x_b, w_out, ffn2_norm, ffn2_w_gate, ffn2_w_up, ffn2_w_down,
              final_norm):
    def trunk(x):
        h = x
        for i in range(DEPTH):
            p = {
                'ffn1_norm': ffn1_norm[i], 'ffn1_w_gate': ffn1_w_gate[i], 'ffn1_w_up': ffn1_w_up[i],
                'ffn1_w_down': ffn1_w_down[i], 'mix_norm': mix_norm[i], 'w_in': w_in[i],
                'q_norm': q_norm[i], 'w_uq': w_uq[i], 'kv_norm': kv_norm[i], 'w_ukv': w_ukv[i],
                'attn_out_norm': attn_out_norm[i], 'shift_mu': shift_mu[i], 'w0': w0[i],
                'w_up': w_up[i], 'a0': a0[i], 'a_up': a_up[i], 'g_up': g_up[i], 'k_k': k_k[i],
                'k_a': k_a[i], 'r_k': r_k[i], 'ln_x_w': ln_x_w[i], 'ln_x_b': ln_x_b[i],
                'w_out': w_out[i], 'ffn2_norm': ffn2_norm[i], 'ffn2_w_gate': ffn2_w_gate[i],
                'ffn2_w_up': ffn2_w_up[i], 'ffn2_w_down': ffn2_w_down[i],
            }
            h = encoder_layer(h, p)
        return rmsnorm(h, final_norm)

    y_prompt = trunk(x_prompt)
    y_sample = trunk(x_sample)
    return (y_prompt, y_sample)
```

```python
import functools
import math

import jax
import jax.numpy as jnp
from jax import lax
from jax.experimental import pallas as pl
from jax.experimental.pallas import tpu as pltpu

F32 = jnp.float32
BF16 = jnp.bfloat16

D_MODEL = 1024
ATT_HEADS = 8
NOPE_DIM = 64
ROPE_DIM = 32
QK_DIM = NOPE_DIM + ROPE_DIM
V_DIM = 64
Q_RANK = 384
KV_RANK = 256
ATT_WIDTH = ATT_HEADS * V_DIM
ROPE_THETA = 10000.0
RWKV_HEADS = 8
RWKV_HEAD_DIM = 64
RWKV_WIDTH = RWKV_HEADS * RWKV_HEAD_DIM
DECAY_LORA = 64
ICLR_LORA = 64
GATE_LORA = 128
D_FF = 2816
NORM_EPS = 1e-6
LN_X_EPS = 64e-5
RWKV_COLS = 3 * RWKV_WIDTH + 2 * DECAY_LORA + 2 * ICLR_LORA + GATE_LORA

LANES = 128
SUBLANES = 8
HEAD_PAD = LANES
QK_WIDTH = ATT_HEADS * HEAD_PAD
FF_CHUNK = 256
N_FF_CHUNKS = D_FF // FF_CHUNK
VMEM_LIMIT = 56 * 1024 * 1024

ROW_TILE = 512
PRE_TILE = 256
ATT_TQ = 256
ATT_TK = 256
SCAN_TB = 256


def _rms(x, g):
    return x * lax.rsqrt(jnp.mean(x * x, axis=-1, keepdims=True) + NORM_EPS) * g


def _dot(a, b):
    return jnp.dot(a, b, preferred_element_type=F32)


def _split_bf16(x):
    hi = x.astype(BF16)
    mid = (x - hi.astype(F32)).astype(BF16)
    return hi, mid


def _segsum(x, ones_bd):
    hi, mid = _split_bf16(x)
    return _dot(hi, ones_bd) + _dot(mid, ones_bd)


def _const_spec(shape):
    return pl.BlockSpec(shape, lambda *_: (0,) * len(shape), pipeline_mode=pl.Buffered(1))


def _params(*sem):
    return pltpu.CompilerParams(dimension_semantics=sem, vmem_limit_bytes=VMEM_LIMIT)


def _ffn_body(x_ref, g_ref, wg_ref, wu_ref, wd_ref, fg_ref, o_ref, acc_ref, *, final):
    x = x_ref[...]
    u = _rms(x, g_ref[...]).astype(BF16)
    acc_ref[...] = jnp.zeros_like(acc_ref)

    def chunk(j, carry):
        gate = _dot(u, wg_ref[j])
        up = _dot(u, wu_ref[j])
        act = (gate * jax.nn.sigmoid(gate) * up).astype(BF16)
        acc_ref[...] += _dot(act, wd_ref[j])
        return carry

    lax.fori_loop(0, N_FF_CHUNKS, chunk, 0)
    h = x + 0.5 * acc_ref[...]
    if final:
        h = _rms(h, fg_ref[...])
    o_ref[...] = h


def _ffn(x, norm_g, wg, wu, wd, final_g, *, final):
    t = x.shape[0]
    tm = min(ROW_TILE, t)
    row = pl.BlockSpec((tm, D_MODEL), lambda i: (i, 0))
    return pl.pallas_call(
        functools.partial(_ffn_body, final=final),
        grid=(t // tm,),
        in_specs=[row, _const_spec((1, D_MODEL)),
                  _const_spec((N_FF_CHUNKS, D_MODEL, FF_CHUNK)),
                  _const_spec((N_FF_CHUNKS, D_MODEL, FF_CHUNK)),
                  _const_spec((N_FF_CHUNKS, FF_CHUNK, D_MODEL)),
                  _const_spec((1, D_MODEL))],
        out_specs=row,
        out_shape=jax.ShapeDtypeStruct((t, D_MODEL), F32),
        scratch_shapes=[pltpu.VMEM((tm, D_MODEL), F32)],
        compiler_params=_params("arbitrary"),
        name="ffn_final" if final else "ffn",
    )(x, norm_g, wg, wu, wd, final_g)


IN_ARR_COLS = Q_RANK + KV_RANK + 2 * LANES + RWKV_COLS
OFF_KR_ARR = Q_RANK + KV_RANK
OFF_RWKV_ARR = OFF_KR_ARR + 2 * LANES


def _inproj_body(h_ref, g_ref, win_ref, qn_ref, wuq_ref, kvn_ref, wkv_ref, cos_ref, sin_ref,
                 q_ref, k_ref, v_ref, zr_ref):
    u = _rms(h_ref[...], g_ref[...]).astype(BF16)
    z = _dot(u, win_ref[...])
    zr_ref[...] = z[:, OFF_RWKV_ARR:]

    cos = cos_ref[...]
    sin = sin_ref[...]
    cos_h = jnp.concatenate([cos] * ATT_HEADS, axis=1)
    sin_h = jnp.concatenate([sin] * ATT_HEADS, axis=1)

    q2 = _dot(_rms(z[:, :Q_RANK], qn_ref[...]).astype(BF16), wuq_ref[...])
    q = (q2[:, :QK_WIDTH] * cos_h + q2[:, QK_WIDTH:] * sin_h) * (QK_DIM ** -0.5)
    q_ref[...] = q.astype(BF16)

    kv = _dot(_rms(z[:, Q_RANK:OFF_KR_ARR], kvn_ref[...]).astype(BF16), wkv_ref[...])
    k_pe = z[:, OFF_KR_ARR:OFF_KR_ARR + LANES] * cos + z[:, OFF_KR_ARR + LANES:OFF_RWKV_ARR] * sin
    k = kv[:, :QK_WIDTH] + jnp.concatenate([k_pe] * ATT_HEADS, axis=1)
    k_ref[...] = k.astype(BF16)
    v_ref[...] = kv[:, QK_WIDTH:].astype(BF16)


def _inproj(h, seq, mix_g, w_in_arr, q_norm, w_uq2, kv_norm, w_kv2, cos_t, sin_t):
    t = h.shape[0]
    tm = min(ROW_TILE, seq)
    per_seq = seq // tm
    row = lambda w: pl.BlockSpec((tm, w), lambda i: (i, 0))
    pos = pl.BlockSpec((tm, LANES), lambda i: (i % per_seq, 0))
    return pl.pallas_call(
        _inproj_body,
        grid=(t // tm,),
        in_specs=[row(D_MODEL), _const_spec((1, D_MODEL)), _const_spec((D_MODEL, IN_ARR_COLS)),
                  _const_spec((1, Q_RANK)), _const_spec((Q_RANK, 2 * QK_WIDTH)),
                  _const_spec((1, KV_RANK)), _const_spec((KV_RANK, 2 * QK_WIDTH)),
                  pos, pos],
        out_specs=[row(QK_WIDTH), row(QK_WIDTH), row(QK_WIDTH), row(RWKV_COLS)],
        out_shape=[jax.ShapeDtypeStruct((t, QK_WIDTH), BF16),
                   jax.ShapeDtypeStruct((t, QK_WIDTH), BF16),
                   jax.ShapeDtypeStruct((t, QK_WIDTH), BF16),
                   jax.ShapeDtypeStruct((t, RWKV_COLS), F32)],
        compiler_params=_params("arbitrary"),
        name="inproj",
    )(h, mix_g, w_in_arr, q_norm, w_uq2, kv_norm, w_kv2, cos_t, sin_t)


def _attn_body(q_ref, k_ref, v_ref, o_ref, *, tk, nk):
    tq = q_ref.shape[1]
    out = jnp.zeros((tq, LANES), F32)
    for hh in range(2):
        lanes = slice(hh * HEAD_PAD, (hh + 1) * HEAD_PAD)
        q = q_ref[0, :, lanes]

        def step(j, carry, lanes=lanes, q=q):
            m, l, acc = carry
            rows = pl.ds(pl.multiple_of(j * tk, tk), tk)
            ks = k_ref[0, rows, lanes]
            vs = v_ref[0, rows, lanes]
            s = lax.dot_general(q, ks, (((1,), (1,)), ((), ())), preferred_element_type=F32)
            m_new = jnp.maximum(m, jnp.max(s, axis=-1, keepdims=True))
            p = jnp.exp(s - m_new)
            alpha = jnp.exp(m - m_new)
            l = alpha * l + jnp.sum(p, axis=-1, keepdims=True)
            acc = alpha * acc + _dot(p.astype(BF16), vs)
            return m_new, l, acc

        init = (jnp.full((tq, 1), -jnp.inf, F32), jnp.zeros((tq, 1), F32), jnp.zeros((tq, LANES), F32))
        _, l, acc = lax.fori_loop(0, nk, step, init)
        out = out + acc / l
    o_ref[0] = out


def _attention(q, k, v):
    b, s, _ = q.shape
    tq = min(ATT_TQ, s)
    tk = min(ATT_TK, s)
    pairs = ATT_HEADS // 2
    return pl.pallas_call(
        functools.partial(_attn_body, tk=tk, nk=s // tk),
        grid=(b, pairs, s // tq),
        in_specs=[pl.BlockSpec((1, tq, 2 * HEAD_PAD), lambda bi, hp, qi: (bi, qi, hp)),
                  pl.BlockSpec((1, s, 2 * HEAD_PAD), lambda bi, hp, qi: (bi, 0, hp),
                               pipeline_mode=pl.Buffered(1)),
                  pl.BlockSpec((1, s, 2 * HEAD_PAD), lambda bi, hp, qi: (bi, 0, hp),
                               pipeline_mode=pl.Buffered(1))],
        out_specs=pl.BlockSpec((1, tq, LANES), lambda bi, hp, qi: (bi, qi, hp)),
        out_shape=jax.ShapeDtypeStruct((b, s, ATT_WIDTH), F32),
        compiler_params=_params("arbitrary", "arbitrary", "arbitrary"),
        name="attention",
    )(q, k, v)


EXP_NEG_HALF = math.exp(-0.5)


def _rwkv_pre_body(z_ref, zp_ref, zn_ref, mu_ref, w0_ref, wup_ref, a0_ref, aup_ref, gup_ref,
                   kk_ref, ka_ref, rk_ref, bd_ref,
                   r_out, v_out, kk_out, wf_out, wb_out, kdf_out, kdb_out, bf_out, bb_out, g_out, bonus_out):
    i = pl.program_id(1)
    n = pl.num_programs(1)
    z = z_ref[0]
    ts = z.shape[0]
    prev_row = jnp.where(i > 0, zp_ref[0, SUBLANES - 1:SUBLANES, :], 0.0)
    next_row = jnp.where(i < n - 1, zn_ref[0, 0:1, :], 0.0)
    rows = lax.broadcasted_iota(jnp.int32, z.shape, 0)
    z_prev = jnp.where(rows == 0, prev_row, pltpu.roll(z, 1, 0))
    z_next = jnp.where(rows == ts - 1, next_row, pltpu.roll(z, ts - 1, 0))
    zs = z + mu_ref[0:1, :] * (z_prev - z) + mu_ref[1:2, :] * (z_next - z)

    w = RWKV_WIDTH
    r = zs[:, 0:w]
    k = zs[:, w:2 * w]
    v = zs[:, 2 * w:3 * w]
    wd = zs[:, 3 * w:3 * w + LANES]
    ad = zs[:, 3 * w + LANES:3 * w + 2 * LANES]
    gd = zs[:, 3 * w + 2 * LANES:]

    lora_w = _dot(jnp.tanh(wd).astype(BF16), wup_ref[...])
    lora_a = _dot(ad.astype(BF16), aup_ref[...])
    g_out[0] = _dot(jax.nn.sigmoid(gd).astype(BF16), gup_ref[...])

    ones_bd = bd_ref[...]
    kkf = k * kk_ref[...]
    kk = kkf / jnp.maximum(jnp.sqrt(_segsum(kkf * kkf, ones_bd)), 1e-12)
    r_out[0] = r
    v_out[0] = v
    kk_out[0] = kk

    k_sum = jnp.zeros_like(k)
    for d, (w_out, kd_out, b_out) in enumerate(((wf_out, kdf_out, bf_out), (wb_out, kdb_out, bb_out))):
        x = w0_ref[d:d + 1, :] + lora_w[:, d * w:(d + 1) * w]
        w_out[0] = jnp.exp(-EXP_NEG_HALF * jax.nn.sigmoid(x))
        a = jax.nn.sigmoid(a0_ref[d:d + 1, :] + lora_a[:, d * w:(d + 1) * w])
        k_d = k * (1.0 + (a - 1.0) * ka_ref[...])
        kd_out[0] = k_d
        b_out[0] = kk * a
        k_sum = k_sum + k_d
    bonus_out[0] = _segsum(r * k_sum * rk_ref[...], ones_bd) * v


def _rwkv_pre(zr, shift_mu, w0, wup_bd, a0, aup_bd, g_up, k_k, k_a, r_k, ones_bd):
    b, s, _ = zr.shape
    ts = min(PRE_TILE, s)
    halo = ts // SUBLANES
    last = s // SUBLANES - 1
    wide = lambda: pl.BlockSpec((1, ts, RWKV_WIDTH), lambda bi, i: (bi, i, 0))
    out = jax.ShapeDtypeStruct((b, s, RWKV_WIDTH), F32)
    return pl.pallas_call(
        _rwkv_pre_body,
        grid=(b, s // ts),
        in_specs=[pl.BlockSpec((1, ts, RWKV_COLS), lambda bi, i: (bi, i, 0)),
                  pl.BlockSpec((1, SUBLANES, RWKV_COLS), lambda bi, i: (bi, jnp.maximum(i * halo - 1, 0), 0)),
                  pl.BlockSpec((1, SUBLANES, RWKV_COLS), lambda bi, i: (bi, jnp.minimum((i + 1) * halo, last), 0)),
                  _const_spec((2, RWKV_COLS)),
                  _const_spec((2, RWKV_WIDTH)), _const_spec((LANES, 2 * RWKV_WIDTH)),
                  _const_spec((2, RWKV_WIDTH)), _const_spec((LANES, 2 * RWKV_WIDTH)),
                  _const_spec((GATE_LORA, RWKV_WIDTH)),
                  _const_spec((1, RWKV_WIDTH)), _const_spec((1, RWKV_WIDTH)), _const_spec((1, RWKV_WIDTH)),
                  _const_spec((RWKV_WIDTH, RWKV_WIDTH))],
        out_specs=[wide() for _ in range(11)],
        out_shape=[out] * 11,
        compiler_params=_params("arbitrary", "arbitrary"),
        name="rwkv_pre",
    )(zr, zr, zr, shift_mu, w0, wup_bd, a0, aup_bd, g_up, k_k, k_a, r_k, ones_bd)


HEAD_PAIRS = RWKV_HEADS // 2
N_CHAINS = 2 * HEAD_PAIRS
STATE_VREGS = RWKV_HEAD_DIM // SUBLANES


def _scan_body(kk_f, w_f, b_f, kd_f, r_f, v_f, kk_b, w_b, b_b, kd_b, r_b, v_b, w2_ref, wo_ref,
               yf_ref, yb_ref, state_ref, *, tb):
    @pl.when(pl.program_id(1) == 0)
    def _():
        state_ref[...] = jnp.zeros_like(state_ref)

    w2 = w2_ref[...]
    wo = wo_ref[...]
    shape3 = (STATE_VREGS, SUBLANES, LANES)
    vrow = lax.broadcasted_iota(jnp.int32, shape3, 0) * SUBLANES + lax.broadcasted_iota(jnp.int32, shape3, 1)
    diag = vrow == lax.broadcasted_iota(jnp.int32, shape3, 2) % RWKV_HEAD_DIM

    def group_sums(x3):
        hi, mid = _split_bf16(x3.reshape(RWKV_HEAD_DIM, LANES))
        return _dot(jnp.concatenate([hi, mid], axis=1), w2).reshape(shape3)

    dirs = ((kk_f, w_f, b_f, kd_f, r_f, v_f, yf_ref), (kk_b, w_b, b_b, kd_b, r_b, v_b, yb_ref))

    n_groups = tb // SUBLANES
    sub = lax.broadcasted_iota(jnp.int32, (SUBLANES, LANES), 0)

    chains = [(d, hp) for d in range(2) for hp in range(HEAD_PAIRS)]

    def token_group(gi, carry):
        rows8 = [pl.ds(pl.multiple_of(g * SUBLANES, SUBLANES), SUBLANES) for g in (gi, n_groups - 1 - gi)]
        tiles = [[ref[0, rows8[d], hp * LANES:(hp + 1) * LANES] for ref in dirs[d][:6]] for d, hp in chains]
        states = [state_ref[c] for c in range(N_CHAINS)]
        y_tiles = [jnp.zeros((SUBLANES, LANES), F32)] * N_CHAINS
        for jj in range(SUBLANES):
            rows = []
            for c, (d, hp) in enumerate(chains):
                j = jj if d == 0 else SUBLANES - 1 - jj
                rows.append([t[j:j + 1, :][None] for t in tiles[c]])
            sa = [group_sums(states[c] * rows[c][0]) for c in range(N_CHAINS)]
            v_col = [group_sums(jnp.where(diag, rows[c][5], 0.0)) for c in range(N_CHAINS)]
            for c in range(N_CHAINS):
                _, w, b, kd, _, _ = rows[c]
                states[c] = states[c] * w - sa[c] * b + v_col[c] * kd
            for c in range(0, N_CHAINS, 2):
                sr = [(states[c + i] * rows[c + i][4]).reshape(RWKV_HEAD_DIM, LANES).astype(BF16) for i in range(2)]
                o2 = _dot(jnp.concatenate(sr, axis=1), wo)
                for i in range(2):
                    d = chains[c + i][0]
                    j = jj if d == 0 else SUBLANES - 1 - jj
                    o = o2[:, i * LANES:(i + 1) * LANES]
                    y = jnp.sum(jnp.where(diag.reshape(RWKV_HEAD_DIM, LANES), o, 0.0), axis=0, keepdims=True)
                    y_tiles[c + i] = jnp.where(sub == j, y, y_tiles[c + i])
        for c, (d, hp) in enumerate(chains):
            state_ref[c] = states[c]
            dirs[d][6][0, rows8[d], hp * LANES:(hp + 1) * LANES] = y_tiles[c]
        return carry

    lax.fori_loop(0, n_groups, token_group, 0)


def _rwkv_scan(r, v, kk, w_f, w_b, kd_f, kd_b, b_f, b_b, w2, wo):
    b, s, _ = r.shape
    tb = min(SCAN_TB, s)
    nb = s // tb
    fwd = pl.BlockSpec((1, tb, RWKV_WIDTH), lambda bi, j: (bi, j, 0))
    bwd = pl.BlockSpec((1, tb, RWKV_WIDTH), lambda bi, j: (bi, nb - 1 - j, 0))
    out = jax.ShapeDtypeStruct((b, s, RWKV_WIDTH), F32)
    return pl.pallas_call(
        functools.partial(_scan_body, tb=tb),
        grid=(b, nb),
        in_specs=[fwd] * 6 + [bwd] * 6 + [_const_spec((2 * LANES, LANES)), _const_spec((2 * LANES, 2 * LANES))],
        out_specs=[fwd, bwd],
        out_shape=[out, out],
        scratch_shapes=[pltpu.VMEM((N_CHAINS, STATE_VREGS, SUBLANES, LANES), F32)],
        compiler_params=_params("arbitrary", "arbitrary"),
        name="rwkv_scan",
    )(kk, w_f, b_f, kd_f, r, v, kk, w_b, b_b, kd_b, r, v, w2, wo)


def _mix_out_body(att_ref, yf_ref, yb_ref, bonus_ref, g_ref, h_ref, an_ref, lnw_ref, lnb_ref, bd_ref,
                  wo_att_ref, wo_tm_ref, o_ref):
    att = _rms(att_ref[...], an_ref[...])
    ones_bd = bd_ref[...]
    y = yf_ref[...] + yb_ref[...]
    inv_n = 1.0 / RWKV_HEAD_DIM
    c = y - _segsum(y, ones_bd) * inv_n
    var = _segsum(c * c, ones_bd) * inv_n
    yn = c * lax.rsqrt(var + LN_X_EPS) * lnw_ref[...] + lnb_ref[...]
    tm = (yn + bonus_ref[...]) * g_ref[...]
    o_ref[...] = (h_ref[...] + _dot(att.astype(BF16), wo_att_ref[...])
                  + _dot(tm.astype(BF16), wo_tm_ref[...]))


def _mix_out(att, yf, yb, bonus, g, h, attn_norm, ln_w, ln_b, ones_bd, wo_att, wo_tm):
    t = h.shape[0]
    tm = min(ROW_TILE, t)
    half = pl.BlockSpec((tm, RWKV_WIDTH), lambda i: (i, 0))
    full = pl.BlockSpec((tm, D_MODEL), lambda i: (i, 0))
    vec = _const_spec((1, RWKV_WIDTH))
    return pl.pallas_call(
        _mix_out_body,
        grid=(t // tm,),
        in_specs=[half] * 5 + [full, vec, vec, vec, _const_spec((RWKV_WIDTH, RWKV_WIDTH)),
                               _const_spec((ATT_WIDTH, D_MODEL)), _const_spec((RWKV_WIDTH, D_MODEL))],
        out_specs=full,
        out_shape=jax.ShapeDtypeStruct((t, D_MODEL), F32),
        compiler_params=_params("arbitrary"),
        name="mix_out",
    )(att, yf, yb, bonus, g, h, attn_norm, ln_w, ln_b, ones_bd, wo_att, wo_tm)


def _rope_tables(seq):
    inv = 1.0 / (ROPE_THETA ** (jnp.arange(0, ROPE_DIM, 2, dtype=F32) / ROPE_DIM))
    ang = jnp.arange(seq, dtype=F32)[:, None] * inv[None, :]
    cos, sin = jnp.cos(ang), jnp.sin(ang)
    pad = jnp.zeros((seq, HEAD_PAD - QK_DIM), F32)
    cos_t = jnp.concatenate([jnp.ones((seq, NOPE_DIM), F32), cos, cos, pad], axis=1)
    sin_t = jnp.concatenate([jnp.zeros((seq, NOPE_DIM), F32), sin, sin, pad], axis=1)
    return cos_t, sin_t


def _rotate_half_cols(w):
    half = ROPE_DIM // 2
    return jnp.concatenate([-w[..., half:], w[..., :half]], axis=-1)


def _pad_head_cols(nope, rope):
    rows = nope.shape[0]
    pad = jnp.zeros((rows, ATT_HEADS, HEAD_PAD - QK_DIM), nope.dtype)
    return jnp.concatenate([nope, rope, pad], axis=-1).reshape(rows, QK_WIDTH)


def _block_diag2(a, b):
    za = jnp.zeros_like(a)
    return jnp.concatenate([jnp.concatenate([a, za], axis=1), jnp.concatenate([za, b], axis=1)], axis=0)


def _prepare(p):
    row = lambda a: a.reshape(1, -1)
    ffn_w = lambda wg, wu, wd: (
        wg.reshape(D_MODEL, N_FF_CHUNKS, FF_CHUNK).transpose(1, 0, 2).astype(BF16),
        wu.reshape(D_MODEL, N_FF_CHUNKS, FF_CHUNK).transpose(1, 0, 2).astype(BF16),
        wd.reshape(N_FF_CHUNKS, FF_CHUNK, D_MODEL).astype(BF16))

    w_in = p['w_in']
    off_ckv = Q_RANK
    off_kr = off_ckv + KV_RANK
    off_rwkv = off_kr + ROPE_DIM
    w_kr = w_in[:, off_kr:off_rwkv]
    kr_pad = lambda w: jnp.concatenate(
        [jnp.zeros((D_MODEL, NOPE_DIM), F32), w, jnp.zeros((D_MODEL, HEAD_PAD - QK_DIM), F32)], axis=1)
    w_in_arr = jnp.concatenate(
        [w_in[:, :off_kr], kr_pad(w_kr), kr_pad(_rotate_half_cols(w_kr)), w_in[:, off_rwkv:]], axis=1).astype(BF16)

    w_uq = p['w_uq'].reshape(Q_RANK, ATT_HEADS, QK_DIM)
    q_nope, q_rope = w_uq[..., :NOPE_DIM], w_uq[..., NOPE_DIM:]
    w_uq2 = jnp.concatenate(
        [_pad_head_cols(q_nope, q_rope), _pad_head_cols(jnp.zeros_like(q_nope), _rotate_half_cols(q_rope))],
        axis=1).astype(BF16)

    w_ukv = p['w_ukv'].reshape(KV_RANK, ATT_HEADS, NOPE_DIM + V_DIM)
    k_nope, v_w = w_ukv[..., :NOPE_DIM], w_ukv[..., NOPE_DIM:]
    w_k = _pad_head_cols(k_nope, jnp.zeros((KV_RANK, ATT_HEADS, ROPE_DIM), F32))
    zero_v = jnp.zeros((KV_RANK, ATT_HEADS // 2, V_DIM), F32)
    w_v = jnp.stack([jnp.concatenate([v_w[:, 0::2], zero_v], axis=-1),
                     jnp.concatenate([zero_v, v_w[:, 1::2]], axis=-1)], axis=2).reshape(KV_RANK, QK_WIDTH)
    w_kv2 = jnp.concatenate([w_k, w_v], axis=1).astype(BF16)

    head = jnp.arange(RWKV_WIDTH) // RWKV_HEAD_DIM
    ones_bd = (head[:, None] == head[None, :]).astype(BF16)
    pair = jnp.arange(LANES) // RWKV_HEAD_DIM
    ones_pair = (pair[:, None] == pair[None, :]).astype(BF16)

    return dict(
        ffn1=(row(p['ffn1_norm']),) + ffn_w(p['ffn1_w_gate'], p['ffn1_w_up'], p['ffn1_w_down']),
        ffn2=(row(p['ffn2_norm']),) + ffn_w(p['ffn2_w_gate'], p['ffn2_w_up'], p['ffn2_w_down']),
        inproj=(row(p['mix_norm']), w_in_arr, row(p['q_norm']), w_uq2, row(p['kv_norm']), w_kv2),
        pre=(p['shift_mu'], p['w0'], _block_diag2(p['w_up'][0], p['w_up'][1]).astype(BF16),
             p['a0'], _block_diag2(p['a_up'][0], p['a_up'][1]).astype(BF16), p['g_up'].astype(BF16),
             row(p['k_k']), row(p['k_a']), row(p['r_k']), ones_bd),
        scan=(jnp.concatenate([ones_pair, ones_pair], axis=0), _block_diag2(ones_pair, ones_pair)),
        out=(row(p['attn_out_norm']), row(p['ln_x_w']), row(p['ln_x_b']), ones_bd,
             p['w_out'][:ATT_WIDTH].astype(BF16), p['w_out'][ATT_WIDTH:].astype(BF16)),
    )


def _trunk(x, w, final_g):
    b, s, _ = x.shape
    x2 = x.reshape(b * s, D_MODEL)
    h = _ffn(x2, *w['ffn1'], final_g, final=False)
    cos_t, sin_t = _rope_tables(s)
    q, k, v, zr = _inproj(h, s, *w['inproj'], cos_t, sin_t)
    att = _attention(q.reshape(b, s, QK_WIDTH), k.reshape(b, s, QK_WIDTH), v.reshape(b, s, QK_WIDTH))
    r, vv, kk, w_f, w_b, kd_f, kd_b, b_f, b_b, g, bonus = _rwkv_pre(zr.reshape(b, s, RWKV_COLS), *w['pre'])
    y_f, y_b = _rwkv_scan(r, vv, kk, w_f, w_b, kd_f, kd_b, b_f, b_b, *w['scan'])
    flat = lambda a: a.reshape(b * s, RWKV_WIDTH)
    h = _mix_out(flat(att), flat(y_f), flat(y_b), flat(bonus), flat(g), h, *w['out'])
    y = _ffn(h, *w['ffn2'], final_g, final=True)
    return y.reshape(b, s, D_MODEL)


def kernel(x_prompt, x_sample, ffn1_norm, ffn1_w_gate, ffn1_w_up, ffn1_w_down, mix_norm, w_in, q_norm, w_uq,
           kv_norm, w_ukv, attn_out_norm, shift_mu, w0, w_up, a0, a_up, g_up, k_k, k_a, r_k, ln_x_w, ln_x_b,
           w_out, ffn2_norm, ffn2_w_gate, ffn2_w_up, ffn2_w_down, final_norm):
    assert ffn1_norm.shape[0] == 1, "single-layer trunk"
    p = dict(ffn1_norm=ffn1_norm[0], ffn1_w_gate=ffn1_w_gate[0], ffn1_w_up=ffn1_w_up[0],
             ffn1_w_down=ffn1_w_down[0], mix_norm=mix_norm[0], w_in=w_in[0], q_norm=q_norm[0], w_uq=w_uq[0],
             kv_norm=kv_norm[0], w_ukv=w_ukv[0], attn_out_norm=attn_out_norm[0], shift_mu=shift_mu[0],
             w0=w0[0], w_up=w_up[0], a0=a0[0], a_up=a_up[0], g_up=g_up[0], k_k=k_k[0], k_a=k_a[0],
             r_k=r_k[0], ln_x_w=ln_x_w[0], ln_x_b=ln_x_b[0], w_out=w_out[0], ffn2_norm=ffn2_norm[0],
             ffn2_w_gate=ffn2_w_gate[0], ffn2_w_up=ffn2_w_up[0], ffn2_w_down=ffn2_w_down[0])
    w = _prepare(p)
    final_g = final_norm.reshape(1, D_MODEL)
    return _trunk(x_prompt, w, final_g), _trunk(x_sample, w, final_g)
```

```python
import functools
import math

import jax
import jax.numpy as jnp
from jax import lax
from jax.experimental import pallas as pl
from jax.experimental.pallas import tpu as pltpu

F32 = jnp.float32
BF16 = jnp.bfloat16

D_MODEL = 1024
ATT_HEADS = 8
NOPE_DIM = 64
ROPE_DIM = 32
QK_DIM = NOPE_DIM + ROPE_DIM
V_DIM = 64
Q_RANK = 384
KV_RANK = 256
ATT_WIDTH = ATT_HEADS * V_DIM
ROPE_THETA = 10000.0
RWKV_HEADS = 8
RWKV_HEAD_DIM = 64
RWKV_WIDTH = RWKV_HEADS * RWKV_HEAD_DIM
DECAY_LORA = 64
ICLR_LORA = 64
GATE_LORA = 128
D_FF = 2816
NORM_EPS = 1e-6
LN_X_EPS = 64e-5
RWKV_COLS = 3 * RWKV_WIDTH + 2 * DECAY_LORA + 2 * ICLR_LORA + GATE_LORA

LANES = 128
SUBLANES = 8
HEAD_PAD = LANES
QK_WIDTH = ATT_HEADS * HEAD_PAD
FF_CHUNK = 256
N_FF_CHUNKS = D_FF // FF_CHUNK
VMEM_LIMIT = 56 * 1024 * 1024

ROW_TILE = 512
PRE_TILE = 256
ATT_TQ = 256
ATT_TK = 1024
ONES_LANE = (V_DIM, 0)
SCAN_TB = 256


def _rms(x, g):
    return x * lax.rsqrt(jnp.mean(x * x, axis=-1, keepdims=True) + NORM_EPS) * g


def _dot(a, b):
    return jnp.dot(a, b, preferred_element_type=F32)


def _split_bf16(x):
    hi = x.astype(BF16)
    mid = (x - hi.astype(F32)).astype(BF16)
    return hi, mid


def _segsum(x, ones_bd):
    hi, mid = _split_bf16(x)
    return _dot(hi, ones_bd) + _dot(mid, ones_bd)


def _const_spec(shape):
    return pl.BlockSpec(shape, lambda *_: (0,) * len(shape), pipeline_mode=pl.Buffered(1))


def _params(*sem):
    return pltpu.CompilerParams(dimension_semantics=sem, vmem_limit_bytes=VMEM_LIMIT)


def _ffn_body(x_ref, g_ref, wg_ref, wu_ref, wd_ref, fg_ref, o_ref, acc_ref, *, final):
    x = x_ref[...]
    u = _rms(x, g_ref[...]).astype(BF16)
    acc_ref[...] = jnp.zeros_like(acc_ref)

    def chunk(j, carry):
        gate = _dot(u, wg_ref[j])
        up = _dot(u, wu_ref[j])
        act = (gate * jax.nn.sigmoid(gate) * up).astype(BF16)
        acc_ref[...] += _dot(act, wd_ref[j])
        return carry

    lax.fori_loop(0, N_FF_CHUNKS, chunk, 0)
    h = x + 0.5 * acc_ref[...]
    if final:
        h = _rms(h, fg_ref[...])
    o_ref[...] = h


def _ffn(x, norm_g, wg, wu, wd, final_g, *, final):
    t = x.shape[0]
    tm = min(ROW_TILE, t)
    row = pl.BlockSpec((tm, D_MODEL), lambda i: (i, 0))
    return pl.pallas_call(
        functools.partial(_ffn_body, final=final),
        grid=(t // tm,),
        in_specs=[row, _const_spec((1, D_MODEL)),
                  _const_spec((N_FF_CHUNKS, D_MODEL, FF_CHUNK)),
                  _const_spec((N_FF_CHUNKS, D_MODEL, FF_CHUNK)),
                  _const_spec((N_FF_CHUNKS, FF_CHUNK, D_MODEL)),
                  _const_spec((1, D_MODEL))],
        out_specs=row,
        out_shape=jax.ShapeDtypeStruct((t, D_MODEL), F32),
        scratch_shapes=[pltpu.VMEM((tm, D_MODEL), F32)],
        compiler_params=_params("arbitrary"),
        name="ffn_final" if final else "ffn",
    )(x, norm_g, wg, wu, wd, final_g)


IN_ARR_COLS = Q_RANK + KV_RANK + 2 * LANES + RWKV_COLS
OFF_KR_ARR = Q_RANK + KV_RANK
OFF_RWKV_ARR = OFF_KR_ARR + 2 * LANES


def _inproj_body(h_ref, g_ref, win_ref, qn_ref, wuq_ref, kvn_ref, wkv_ref, cos_ref, sin_ref,
                 q_ref, k_ref, v_ref, zr_ref):
    u = _rms(h_ref[...], g_ref[...]).astype(BF16)
    z = _dot(u, win_ref[...])
    zr_ref[...] = z[:, OFF_RWKV_ARR:]

    cos = cos_ref[...]
    sin = sin_ref[...]
    cos_h = jnp.concatenate([cos] * ATT_HEADS, axis=1)
    sin_h = jnp.concatenate([sin] * ATT_HEADS, axis=1)

    q2 = _dot(_rms(z[:, :Q_RANK], qn_ref[...]).astype(BF16), wuq_ref[...])
    q = (q2[:, :QK_WIDTH] * cos_h + q2[:, QK_WIDTH:] * sin_h) * (QK_DIM ** -0.5 * math.log2(math.e))
    q_ref[...] = q.astype(BF16)

    kv = _dot(_rms(z[:, Q_RANK:OFF_KR_ARR], kvn_ref[...]).astype(BF16), wkv_ref[...])
    k_pe = z[:, OFF_KR_ARR:OFF_KR_ARR + LANES] * cos + z[:, OFF_KR_ARR + LANES:OFF_RWKV_ARR] * sin
    k = kv[:, :QK_WIDTH] + jnp.concatenate([k_pe] * ATT_HEADS, axis=1)
    k_ref[...] = k.astype(BF16)
    v = kv[:, QK_WIDTH:]
    col = lax.broadcasted_iota(jnp.int32, v.shape, 1) % (2 * HEAD_PAD)
    is_ones = (col == ONES_LANE[0]) | (col == HEAD_PAD + ONES_LANE[1])
    v_ref[...] = jnp.where(is_ones, 1.0, v).astype(BF16)


def _inproj(h, seq, mix_g, w_in_arr, q_norm, w_uq2, kv_norm, w_kv2, cos_t, sin_t):
    t = h.shape[0]
    tm = min(ROW_TILE, seq)
    per_seq = seq // tm
    row = lambda w: pl.BlockSpec((tm, w), lambda i: (i, 0))
    pos = pl.BlockSpec((tm, LANES), lambda i: (i % per_seq, 0))
    return pl.pallas_call(
        _inproj_body,
        grid=(t // tm,),
        in_specs=[row(D_MODEL), _const_spec((1, D_MODEL)), _const_spec((D_MODEL, IN_ARR_COLS)),
                  _const_spec((1, Q_RANK)), _const_spec((Q_RANK, 2 * QK_WIDTH)),
                  _const_spec((1, KV_RANK)), _const_spec((KV_RANK, 2 * QK_WIDTH)),
                  pos, pos],
        out_specs=[row(QK_WIDTH), row(QK_WIDTH), row(QK_WIDTH), row(RWKV_COLS)],
        out_shape=[jax.ShapeDtypeStruct((t, QK_WIDTH), BF16),
                   jax.ShapeDtypeStruct((t, QK_WIDTH), BF16),
                   jax.ShapeDtypeStruct((t, QK_WIDTH), BF16),
                   jax.ShapeDtypeStruct((t, RWKV_COLS), F32)],
        compiler_params=_params("arbitrary"),
        name="inproj",
    )(h, mix_g, w_in_arr, q_norm, w_uq2, kv_norm, w_kv2, cos_t, sin_t)


def _attn_body(q_ref, k_ref, v_ref, o_ref, *, tk, nk):
    tq = q_ref.shape[1]
    head_lanes = [slice(hh * HEAD_PAD, (hh + 1) * HEAD_PAD) for hh in range(2)]
    qs = [q_ref[0, :, lanes] for lanes in head_lanes]

    def step(j, carry):
        rows = pl.ds(pl.multiple_of(j * tk, tk), tk)
        new = []
        for hh in range(2):
            m, acc = carry[hh]
            ks = k_ref[0, rows, head_lanes[hh]]
            vs = v_ref[0, rows, head_lanes[hh]]
            s = lax.dot_general(qs[hh], ks, (((1,), (1,)), ((), ())), preferred_element_type=F32)
            m_new = jnp.maximum(m, jnp.max(s, axis=-1, keepdims=True))
            p = jnp.exp2(s - m_new)
            acc = jnp.exp2(m - m_new) * acc + _dot(p.astype(BF16), vs)
            new.append((m_new, acc))
        return tuple(new)

    init = (jnp.full((tq, 1), -jnp.inf, F32), jnp.zeros((tq, LANES), F32))
    res = lax.fori_loop(0, nk, step, (init, init))
    lane = lax.broadcasted_iota(jnp.int32, (tq, LANES), 1)
    out = jnp.zeros((tq, LANES), F32)
    for hh in range(2):
        acc = res[hh][1]
        ones_lane = ONES_LANE[hh]
        is_v = (lane < V_DIM) if hh == 0 else (lane >= V_DIM)
        out = out + jnp.where(is_v, acc / acc[:, ones_lane:ones_lane + 1], 0.0)
    o_ref[0] = out


def _attention(q, k, v):
    b, s, _ = q.shape
    tq = min(ATT_TQ, s)
    tk = min(ATT_TK, s)
    pairs = ATT_HEADS // 2
    return pl.pallas_call(
        functools.partial(_attn_body, tk=tk, nk=s // tk),
        grid=(b, pairs, s // tq),
        in_specs=[pl.BlockSpec((1, tq, 2 * HEAD_PAD), lambda bi, hp, qi: (bi, qi, hp)),
                  pl.BlockSpec((1, s, 2 * HEAD_PAD), lambda bi, hp, qi: (bi, 0, hp),
                               pipeline_mode=pl.Buffered(1)),
                  pl.BlockSpec((1, s, 2 * HEAD_PAD), lambda bi, hp, qi: (bi, 0, hp),
                               pipeline_mode=pl.Buffered(1))],
        out_specs=pl.BlockSpec((1, tq, LANES), lambda bi, hp, qi: (bi, qi, hp)),
        out_shape=jax.ShapeDtypeStruct((b, s, ATT_WIDTH), F32),
        compiler_params=_params("arbitrary", "arbitrary", "arbitrary"),
        name="attention",
    )(q, k, v)


EXP_NEG_HALF = math.exp(-0.5)


def _rwkv_pre_body(z_ref, zp_ref, zn_ref, mu_ref, w0_ref, wup_ref, a0_ref, aup_ref, gup_ref,
                   kk_ref, ka_ref, rk_ref, bd_ref,
                   r_out, v_out, kk_out, wf_out, wb_out, kdf_out, kdb_out, bf_out, bb_out, g_out, bonus_out):
    i = pl.program_id(1)
    n = pl.num_programs(1)
    z = z_ref[0]
    ts = z.shape[0]
    prev_row = jnp.where(i > 0, zp_ref[0, SUBLANES - 1:SUBLANES, :], 0.0)
    next_row = jnp.where(i < n - 1, zn_ref[0, 0:1, :], 0.0)
    rows = lax.broadcasted_iota(jnp.int32, z.shape, 0)
    z_prev = jnp.where(rows == 0, prev_row, pltpu.roll(z, 1, 0))
    z_next = jnp.where(rows == ts - 1, next_row, pltpu.roll(z, ts - 1, 0))
    zs = z + mu_ref[0:1, :] * (z_prev - z) + mu_ref[1:2, :] * (z_next - z)

    w = RWKV_WIDTH
    r = zs[:, 0:w]
    k = zs[:, w:2 * w]
    v = zs[:, 2 * w:3 * w]
    wd = zs[:, 3 * w:3 * w + LANES]
    ad = zs[:, 3 * w + LANES:3 * w + 2 * LANES]
    gd = zs[:, 3 * w + 2 * LANES:]

    lora_w = _dot(jnp.tanh(wd).astype(BF16), wup_ref[...])
    lora_a = _dot(ad.astype(BF16), aup_ref[...])
    g_out[0] = _dot(jax.nn.sigmoid(gd).astype(BF16), gup_ref[...])

    ones_bd = bd_ref[...]
    kkf = k * kk_ref[...]
    kk = kkf / jnp.maximum(jnp.sqrt(_segsum(kkf * kkf, ones_bd)), 1e-12)
    r_out[0] = r
    v_out[0] = v
    kk_out[0] = kk

    k_sum = jnp.zeros_like(k)
    for d, (w_out, kd_out, b_out) in enumerate(((wf_out, kdf_out, bf_out), (wb_out, kdb_out, bb_out))):
        x = w0_ref[d:d + 1, :] + lora_w[:, d * w:(d + 1) * w]
        w_out[0] = jnp.exp(-EXP_NEG_HALF * jax.nn.sigmoid(x))
        a = jax.nn.sigmoid(a0_ref[d:d + 1, :] + lora_a[:, d * w:(d + 1) * w])
        k_d = k * (1.0 + (a - 1.0) * ka_ref[...])
        kd_out[0] = k_d
        b_out[0] = kk * a
        k_sum = k_sum + k_d
    bonus_out[0] = _segsum(r * k_sum * rk_ref[...], ones_bd) * v


def _rwkv_pre(zr, shift_mu, w0, wup_bd, a0, aup_bd, g_up, k_k, k_a, r_k, ones_bd):
    b, s, _ = zr.shape
    ts = min(PRE_TILE, s)
    halo = ts // SUBLANES
    last = s // SUBLANES - 1
    wide = lambda: pl.BlockSpec((1, ts, RWKV_WIDTH), lambda bi, i: (bi, i, 0))
    out = jax.ShapeDtypeStruct((b, s, RWKV_WIDTH), F32)
    return pl.pallas_call(
        _rwkv_pre_body,
        grid=(b, s // ts),
        in_specs=[pl.BlockSpec((1, ts, RWKV_COLS), lambda bi, i: (bi, i, 0)),
                  pl.BlockSpec((1, SUBLANES, RWKV_COLS), lambda bi, i: (bi, jnp.maximum(i * halo - 1, 0), 0)),
                  pl.BlockSpec((1, SUBLANES, RWKV_COLS), lambda bi, i: (bi, jnp.minimum((i + 1) * halo, last), 0)),
                  _const_spec((2, RWKV_COLS)),
                  _const_spec((2, RWKV_WIDTH)), _const_spec((LANES, 2 * RWKV_WIDTH)),
                  _const_spec((2, RWKV_WIDTH)), _const_spec((LANES, 2 * RWKV_WIDTH)),
                  _const_spec((GATE_LORA, RWKV_WIDTH)),
                  _const_spec((1, RWKV_WIDTH)), _const_spec((1, RWKV_WIDTH)), _const_spec((1, RWKV_WIDTH)),
                  _const_spec((RWKV_WIDTH, RWKV_WIDTH))],
        out_specs=[wide() for _ in range(11)],
        out_shape=[out] * 11,
        compiler_params=_params("arbitrary", "arbitrary"),
        name="rwkv_pre",
    )(zr, zr, zr, shift_mu, w0, wup_bd, a0, aup_bd, g_up, k_k, k_a, r_k, ones_bd)


HEAD_PAIRS = RWKV_HEADS // 2
N_CHAINS = 2 * HEAD_PAIRS
STATE_VREGS = RWKV_HEAD_DIM // SUBLANES


def _scan_body(kk_f, w_f, b_f, kd_f, r_f, v_f, kk_b, w_b, b_b, kd_b, r_b, v_b, w2_ref, wo_ref,
               yf_ref, yb_ref, state_ref, *, tb):
    @pl.when(pl.program_id(1) == 0)
    def _():
        state_ref[...] = jnp.zeros_like(state_ref)

    w2 = w2_ref[...]
    wo = wo_ref[...]
    shape3 = (STATE_VREGS, SUBLANES, LANES)
    vrow = lax.broadcasted_iota(jnp.int32, shape3, 0) * SUBLANES + lax.broadcasted_iota(jnp.int32, shape3, 1)
    diag = vrow == lax.broadcasted_iota(jnp.int32, shape3, 2) % RWKV_HEAD_DIM

    def group_sums(x3):
        hi, mid = _split_bf16(x3.reshape(RWKV_HEAD_DIM, LANES))
        return _dot(jnp.concatenate([hi, mid], axis=1), w2).reshape(shape3)

    dirs = ((kk_f, w_f, b_f, kd_f, r_f, v_f, yf_ref), (kk_b, w_b, b_b, kd_b, r_b, v_b, yb_ref))

    n_groups = tb // SUBLANES
    sub = lax.broadcasted_iota(jnp.int32, (SUBLANES, LANES), 0)

    chains = [(d, hp) for d in range(2) for hp in range(HEAD_PAIRS)]

    def token_group(gi, carry):
        rows8 = [pl.ds(pl.multiple_of(g * SUBLANES, SUBLANES), SUBLANES) for g in (gi, n_groups - 1 - gi)]
        tiles = [[ref[0, rows8[d], hp * LANES:(hp + 1) * LANES] for ref in dirs[d][:6]] for d, hp in chains]
        states = [state_ref[c] for c in range(N_CHAINS)]
        y_tiles = [jnp.zeros((SUBLANES, LANES), F32)] * N_CHAINS
        for jj in range(SUBLANES):
            rows = []
            for c, (d, hp) in enumerate(chains):
                j = jj if d == 0 else SUBLANES - 1 - jj
                rows.append([t[j:j + 1, :][None] for t in tiles[c]])
            sa = [group_sums(states[c] * rows[c][0]) for c in range(N_CHAINS)]
            v_col = [group_sums(jnp.where(diag, rows[c][5], 0.0)) for c in range(N_CHAINS)]
            for c in range(N_CHAINS):
                _, w, b, kd, _, _ = rows[c]
                states[c] = states[c] * w - sa[c] * b + v_col[c] * kd
            for c in range(0, N_CHAINS, 2):
                sr = [(states[c + i] * rows[c + i][4]).reshape(RWKV_HEAD_DIM, LANES).astype(BF16) for i in range(2)]
                o2 = _dot(jnp.concatenate(sr, axis=1), wo)
                for i in range(2):
                    d = chains[c + i][0]
                    j = jj if d == 0 else SUBLANES - 1 - jj
                    o = o2[:, i * LANES:(i + 1) * LANES]
                    y = jnp.sum(jnp.where(diag.reshape(RWKV_HEAD_DIM, LANES), o, 0.0), axis=0, keepdims=True)
                    y_tiles[c + i] = jnp.where(sub == j, y, y_tiles[c + i])
        for c, (d, hp) in enumerate(chains):
            state_ref[c] = states[c]
            dirs[d][6][0, rows8[d], hp * LANES:(hp + 1) * LANES] = y_tiles[c]
        return carry

    lax.fori_loop(0, n_groups, token_group, 0)


def _rwkv_scan(r, v, kk, w_f, w_b, kd_f, kd_b, b_f, b_b, w2, wo):
    b, s, _ = r.shape
    tb = min(SCAN_TB, s)
    nb = s // tb
    fwd = pl.BlockSpec((1, tb, RWKV_WIDTH), lambda bi, j: (bi, j, 0))
    bwd = pl.BlockSpec((1, tb, RWKV_WIDTH), lambda bi, j: (bi, nb - 1 - j, 0))
    out = jax.ShapeDtypeStruct((b, s, RWKV_WIDTH), F32)
    return pl.pallas_call(
        functools.partial(_scan_body, tb=tb),
        grid=(b, nb),
        in_specs=[fwd] * 6 + [bwd] * 6 + [_const_spec((2 * LANES, LANES)), _const_spec((2 * LANES, 2 * LANES))],
        out_specs=[fwd, bwd],
        out_shape=[out, out],
        scratch_shapes=[pltpu.VMEM((N_CHAINS, STATE_VREGS, SUBLANES, LANES), F32)],
        compiler_params=_params("arbitrary", "arbitrary"),
        name="rwkv_scan",
    )(kk, w_f, b_f, kd_f, r, v, kk, w_b, b_b, kd_b, r, v, w2, wo)


def _mix_out_body(att_ref, yf_ref, yb_ref, bonus_ref, g_ref, h_ref, an_ref, lnw_ref, lnb_ref, bd_ref,
                  wo_att_ref, wo_tm_ref, o_ref):
    att = _rms(att_ref[...], an_ref[...])
    ones_bd = bd_ref[...]
    y = yf_ref[...] + yb_ref[...]
    inv_n = 1.0 / RWKV_HEAD_DIM
    c = y - _segsum(y, ones_bd) * inv_n
    var = _segsum(c * c, ones_bd) * inv_n
    yn = c * lax.rsqrt(var + LN_X_EPS) * lnw_ref[...] + lnb_ref[...]
    tm = (yn + bonus_ref[...]) * g_ref[...]
    o_ref[...] = (h_ref[...] + _dot(att.astype(BF16), wo_att_ref[...])
                  + _dot(tm.astype(BF16), wo_tm_ref[...]))


def _mix_out(att, yf, yb, bonus, g, h, attn_norm, ln_w, ln_b, ones_bd, wo_att, wo_tm):
    t = h.shape[0]
    tm = min(ROW_TILE, t)
    half = pl.BlockSpec((tm, RWKV_WIDTH), lambda i: (i, 0))
    full = pl.BlockSpec((tm, D_MODEL), lambda i: (i, 0))
    vec = _const_spec((1, RWKV_WIDTH))
    return pl.pallas_call(
        _mix_out_body,
        grid=(t // tm,),
        in_specs=[half] * 5 + [full, vec, vec, vec, _const_spec((RWKV_WIDTH, RWKV_WIDTH)),
                               _const_spec((ATT_WIDTH, D_MODEL)), _const_spec((RWKV_WIDTH, D_MODEL))],
        out_specs=full,
        out_shape=jax.ShapeDtypeStruct((t, D_MODEL), F32),
        compiler_params=_params("arbitrary"),
        name="mix_out",
    )(att, yf, yb, bonus, g, h, attn_norm, ln_w, ln_b, ones_bd, wo_att, wo_tm)


def _rope_tables(seq):
    inv = 1.0 / (ROPE_THETA ** (jnp.arange(0, ROPE_DIM, 2, dtype=F32) / ROPE_DIM))
    ang = jnp.arange(seq, dtype=F32)[:, None] * inv[None, :]
    cos, sin = jnp.cos(ang), jnp.sin(ang)
    pad = jnp.zeros((seq, HEAD_PAD - QK_DIM), F32)
    cos_t = jnp.concatenate([jnp.ones((seq, NOPE_DIM), F32), cos, cos, pad], axis=1)
    sin_t = jnp.concatenate([jnp.zeros((seq, NOPE_DIM), F32), sin, sin, pad], axis=1)
    return cos_t, sin_t


def _rotate_half_cols(w):
    half = ROPE_DIM // 2
    return jnp.concatenate([-w[..., half:], w[..., :half]], axis=-1)


def _pad_head_cols(nope, rope):
    rows = nope.shape[0]
    pad = jnp.zeros((rows, ATT_HEADS, HEAD_PAD - QK_DIM), nope.dtype)
    return jnp.concatenate([nope, rope, pad], axis=-1).reshape(rows, QK_WIDTH)


def _block_diag2(a, b):
    za = jnp.zeros_like(a)
    return jnp.concatenate([jnp.concatenate([a, za], axis=1), jnp.concatenate([za, b], axis=1)], axis=0)


def _prepare(p):
    row = lambda a: a.reshape(1, -1)
    ffn_w = lambda wg, wu, wd: (
        wg.reshape(D_MODEL, N_FF_CHUNKS, FF_CHUNK).transpose(1, 0, 2).astype(BF16),
        wu.reshape(D_MODEL, N_FF_CHUNKS, FF_CHUNK).transpose(1, 0, 2).astype(BF16),
        wd.reshape(N_FF_CHUNKS, FF_CHUNK, D_MODEL).astype(BF16))

    w_in = p['w_in']
    off_ckv = Q_RANK
    off_kr = off_ckv + KV_RANK
    off_rwkv = off_kr + ROPE_DIM
    w_kr = w_in[:, off_kr:off_rwkv]
    kr_pad = lambda w: jnp.concatenate(
        [jnp.zeros((D_MODEL, NOPE_DIM), F32), w, jnp.zeros((D_MODEL, HEAD_PAD - QK_DIM), F32)], axis=1)
    w_in_arr = jnp.concatenate(
        [w_in[:, :off_kr], kr_pad(w_kr), kr_pad(_rotate_half_cols(w_kr)), w_in[:, off_rwkv:]], axis=1).astype(BF16)

    w_uq = p['w_uq'].reshape(Q_RANK, ATT_HEADS, QK_DIM)
    q_nope, q_rope = w_uq[..., :NOPE_DIM], w_uq[..., NOPE_DIM:]
    w_uq2 = jnp.concatenate(
        [_pad_head_cols(q_nope, q_rope), _pad_head_cols(jnp.zeros_like(q_nope), _rotate_half_cols(q_rope))],
        axis=1).astype(BF16)

    w_ukv = p['w_ukv'].reshape(KV_RANK, ATT_HEADS, NOPE_DIM + V_DIM)
    k_nope, v_w = w_ukv[..., :NOPE_DIM], w_ukv[..., NOPE_DIM:]
    w_k = _pad_head_cols(k_nope, jnp.zeros((KV_RANK, ATT_HEADS, ROPE_DIM), F32))
    zero_v = jnp.zeros((KV_RANK, ATT_HEADS // 2, V_DIM), F32)
    w_v = jnp.stack([jnp.concatenate([v_w[:, 0::2], zero_v], axis=-1),
                     jnp.concatenate([zero_v, v_w[:, 1::2]], axis=-1)], axis=2).reshape(KV_RANK, QK_WIDTH)
    w_kv2 = jnp.concatenate([w_k, w_v], axis=1).astype(BF16)

    head = jnp.arange(RWKV_WIDTH) // RWKV_HEAD_DIM
    ones_bd = (head[:, None] == head[None, :]).astype(BF16)
    pair = jnp.arange(LANES) // RWKV_HEAD_DIM
    ones_pair = (pair[:, None] == pair[None, :]).astype(BF16)

    return dict(
        ffn1=(row(p['ffn1_norm']),) + ffn_w(p['ffn1_w_gate'], p['ffn1_w_up'], p['ffn1_w_down']),
        ffn2=(row(p['ffn2_norm']),) + ffn_w(p['ffn2_w_gate'], p['ffn2_w_up'], p['ffn2_w_down']),
        inproj=(row(p['mix_norm']), w_in_arr, row(p['q_norm']), w_uq2, row(p['kv_norm']), w_kv2),
        pre=(p['shift_mu'], p['w0'], _block_diag2(p['w_up'][0], p['w_up'][1]).astype(BF16),
             p['a0'], _block_diag2(p['a_up'][0], p['a_up'][1]).astype(BF16), p['g_up'].astype(BF16),
             row(p['k_k']), row(p['k_a']), row(p['r_k']), ones_bd),
        scan=(jnp.concatenate([ones_pair, ones_pair], axis=0), _block_diag2(ones_pair, ones_pair)),
        out=(row(p['attn_out_norm']), row(p['ln_x_w']), row(p['ln_x_b']), ones_bd,
             p['w_out'][:ATT_WIDTH].astype(BF16), p['w_out'][ATT_WIDTH:].astype(BF16)),
    )


def _trunk(x, w, final_g):
    b, s, _ = x.shape
    x2 = x.reshape(b * s, D_MODEL)
    h = _ffn(x2, *w['ffn1'], final_g, final=False)
    cos_t, sin_t = _rope_tables(s)
    q, k, v, zr = _inproj(h, s, *w['inproj'], cos_t, sin_t)
    att = _attention(q.reshape(b, s, QK_WIDTH), k.reshape(b, s, QK_WIDTH), v.reshape(b, s, QK_WIDTH))
    r, vv, kk, w_f, w_b, kd_f, kd_b, b_f, b_b, g, bonus = _rwkv_pre(zr.reshape(b, s, RWKV_COLS), *w['pre'])
    y_f, y_b = _rwkv_scan(r, vv, kk, w_f, w_b, kd_f, kd_b, b_f, b_b, *w['scan'])
    flat = lambda a: a.reshape(b * s, RWKV_WIDTH)
    h = _mix_out(flat(att), flat(y_f), flat(y_b), flat(bonus), flat(g), h, *w['out'])
    y = _ffn(h, *w['ffn2'], final_g, final=True)
    return y.reshape(b, s, D_MODEL)


def kernel(x_prompt, x_sample, ffn1_norm, ffn1_w_gate, ffn1_w_up, ffn1_w_down, mix_norm, w_in, q_norm, w_uq,
           kv_norm, w_ukv, attn_out_norm, shift_mu, w0, w_up, a0, a_up, g_up, k_k, k_a, r_k, ln_x_w, ln_x_b,
           w_out, ffn2_norm, ffn2_w_gate, ffn2_w_up, ffn2_w_down, final_norm):
    assert ffn1_norm.shape[0] == 1, "single-layer trunk"
    p = dict(ffn1_norm=ffn1_norm[0], ffn1_w_gate=ffn1_w_gate[0], ffn1_w_up=ffn1_w_up[0],
             ffn1_w_down=ffn1_w_down[0], mix_norm=mix_norm[0], w_in=w_in[0], q_norm=q_norm[0], w_uq=w_uq[0],
             kv_norm=kv_norm[0], w_ukv=w_ukv[0], attn_out_norm=attn_out_norm[0], shift_mu=shift_mu[0],
             w0=w0[0], w_up=w_up[0], a0=a0[0], a_up=a_up[0], g_up=g_up[0], k_k=k_k[0], k_a=k_a[0],
             r_k=r_k[0], ln_x_w=ln_x_w[0], ln_x_b=ln_x_b[0], w_out=w_out[0], ffn2_norm=ffn2_norm[0],
             ffn2_w_gate=ffn2_w_gate[0], ffn2_w_up=ffn2_w_up[0], ffn2_w_down=ffn2_w_down[0])
    w = _prepare(p)
    final_g = final_norm.reshape(1, D_MODEL)
    return _trunk(x_prompt, w, final_g), _trunk(x_sample, w, final_g)
```

```python
import functools
import math

import jax
import jax.numpy as jnp
from jax import lax
from jax.experimental import pallas as pl
from jax.experimental.pallas import tpu as pltpu

F32 = jnp.float32
BF16 = jnp.bfloat16

D_MODEL = 1024
ATT_HEADS = 8
NOPE_DIM = 64
ROPE_DIM = 32
QK_DIM = NOPE_DIM + ROPE_DIM
V_DIM = 64
Q_RANK = 384
KV_RANK = 256
ATT_WIDTH = ATT_HEADS * V_DIM
ROPE_THETA = 10000.0
RWKV_HEADS = 8
RWKV_HEAD_DIM = 64
RWKV_WIDTH = RWKV_HEADS * RWKV_HEAD_DIM
DECAY_LORA = 64
ICLR_LORA = 64
GATE_LORA = 128
D_FF = 2816
NORM_EPS = 1e-6
LN_X_EPS = 64e-5
RWKV_COLS = 3 * RWKV_WIDTH + 2 * DECAY_LORA + 2 * ICLR_LORA + GATE_LORA

LANES = 128
SUBLANES = 8
HEAD_PAD = LANES
QK_WIDTH = ATT_HEADS * HEAD_PAD
FF_CHUNK = 256
N_FF_CHUNKS = D_FF // FF_CHUNK
VMEM_LIMIT = 56 * 1024 * 1024

ROW_TILE = 512
PRE_TILE = 256
ATT_TQ = 256
ATT_TK = 1024
ONES_LANE = (V_DIM, 0)
SCAN_TB = 128


def _rms(x, g):
    return x * lax.rsqrt(jnp.mean(x * x, axis=-1, keepdims=True) + NORM_EPS) * g


def _dot(a, b):
    return jnp.dot(a, b, preferred_element_type=F32)


def _split_bf16(x):
    hi = x.astype(BF16)
    mid = (x - hi.astype(F32)).astype(BF16)
    return hi, mid


def _segsum(x, ones_bd):
    hi, mid = _split_bf16(x)
    return _dot(hi, ones_bd) + _dot(mid, ones_bd)


def _const_spec(shape):
    return pl.BlockSpec(shape, lambda *_: (0,) * len(shape), pipeline_mode=pl.Buffered(1))


def _params(*sem):
    return pltpu.CompilerParams(dimension_semantics=sem, vmem_limit_bytes=VMEM_LIMIT)


def _ffn_body(x_ref, g_ref, wg_ref, wu_ref, wd_ref, fg_ref, o_ref, acc_ref, *, final):
    x = x_ref[...]
    u = _rms(x, g_ref[...]).astype(BF16)
    acc_ref[...] = jnp.zeros_like(acc_ref)

    def chunk(j, carry):
        gate = _dot(u, wg_ref[j])
        up = _dot(u, wu_ref[j])
        act = (gate * jax.nn.sigmoid(gate) * up).astype(BF16)
        acc_ref[...] += _dot(act, wd_ref[j])
        return carry

    lax.fori_loop(0, N_FF_CHUNKS, chunk, 0)
    h = x + 0.5 * acc_ref[...]
    if final:
        h = _rms(h, fg_ref[...])
    o_ref[...] = h


def _ffn(x, norm_g, wg, wu, wd, final_g, *, final):
    t = x.shape[0]
    tm = min(ROW_TILE, t)
    row = pl.BlockSpec((tm, D_MODEL), lambda i: (i, 0))
    return pl.pallas_call(
        functools.partial(_ffn_body, final=final),
        grid=(t // tm,),
        in_specs=[row, _const_spec((1, D_MODEL)),
                  _const_spec((N_FF_CHUNKS, D_MODEL, FF_CHUNK)),
                  _const_spec((N_FF_CHUNKS, D_MODEL, FF_CHUNK)),
                  _const_spec((N_FF_CHUNKS, FF_CHUNK, D_MODEL)),
                  _const_spec((1, D_MODEL))],
        out_specs=row,
        out_shape=jax.ShapeDtypeStruct((t, D_MODEL), F32),
        scratch_shapes=[pltpu.VMEM((tm, D_MODEL), F32)],
        compiler_params=_params("arbitrary"),
        name="ffn_final" if final else "ffn",
    )(x, norm_g, wg, wu, wd, final_g)


IN_ARR_COLS = Q_RANK + KV_RANK + 2 * LANES + RWKV_COLS
OFF_KR_ARR = Q_RANK + KV_RANK
OFF_RWKV_ARR = OFF_KR_ARR + 2 * LANES


def _inproj_body(h_ref, g_ref, win_ref, qn_ref, wuq_ref, kvn_ref, wkv_ref, cos_ref, sin_ref,
                 q_ref, k_ref, v_ref, zr_ref):
    u = _rms(h_ref[...], g_ref[...]).astype(BF16)
    z = _dot(u, win_ref[...])
    zr_ref[...] = z[:, OFF_RWKV_ARR:]

    cos = cos_ref[...]
    sin = sin_ref[...]
    cos_h = jnp.concatenate([cos] * ATT_HEADS, axis=1)
    sin_h = jnp.concatenate([sin] * ATT_HEADS, axis=1)

    q2 = _dot(_rms(z[:, :Q_RANK], qn_ref[...]).astype(BF16), wuq_ref[...])
    q = (q2[:, :QK_WIDTH] * cos_h + q2[:, QK_WIDTH:] * sin_h) * (QK_DIM ** -0.5 * math.log2(math.e))
    q_ref[...] = q.astype(BF16)

    kv = _dot(_rms(z[:, Q_RANK:OFF_KR_ARR], kvn_ref[...]).astype(BF16), wkv_ref[...])
    k_pe = z[:, OFF_KR_ARR:OFF_KR_ARR + LANES] * cos + z[:, OFF_KR_ARR + LANES:OFF_RWKV_ARR] * sin
    k = kv[:, :QK_WIDTH] + jnp.concatenate([k_pe] * ATT_HEADS, axis=1)
    k_ref[...] = k.astype(BF16)
    v = kv[:, QK_WIDTH:]
    col = lax.broadcasted_iota(jnp.int32, v.shape, 1) % (2 * HEAD_PAD)
    is_ones = (col == ONES_LANE[0]) | (col == HEAD_PAD + ONES_LANE[1])
    v_ref[...] = jnp.where(is_ones, 1.0, v).astype(BF16)


def _inproj(h, seq, mix_g, w_in_arr, q_norm, w_uq2, kv_norm, w_kv2, cos_t, sin_t):
    t = h.shape[0]
    tm = min(ROW_TILE, seq)
    per_seq = seq // tm
    row = lambda w: pl.BlockSpec((tm, w), lambda i: (i, 0))
    pos = pl.BlockSpec((tm, LANES), lambda i: (i % per_seq, 0))
    return pl.pallas_call(
        _inproj_body,
        grid=(t // tm,),
        in_specs=[row(D_MODEL), _const_spec((1, D_MODEL)), _const_spec((D_MODEL, IN_ARR_COLS)),
                  _const_spec((1, Q_RANK)), _const_spec((Q_RANK, 2 * QK_WIDTH)),
                  _const_spec((1, KV_RANK)), _const_spec((KV_RANK, 2 * QK_WIDTH)),
                  pos, pos],
        out_specs=[row(QK_WIDTH), row(QK_WIDTH), row(QK_WIDTH), row(RWKV_COLS)],
        out_shape=[jax.ShapeDtypeStruct((t, QK_WIDTH), BF16),
                   jax.ShapeDtypeStruct((t, QK_WIDTH), BF16),
                   jax.ShapeDtypeStruct((t, QK_WIDTH), BF16),
                   jax.ShapeDtypeStruct((t, RWKV_COLS), F32)],
        compiler_params=_params("arbitrary"),
        name="inproj",
    )(h, mix_g, w_in_arr, q_norm, w_uq2, kv_norm, w_kv2, cos_t, sin_t)


def _attn_body(q_ref, k_ref, v_ref, o_ref, *, tk, nk):
    tq = q_ref.shape[1]
    head_lanes = [slice(hh * HEAD_PAD, (hh + 1) * HEAD_PAD) for hh in range(2)]
    qs = [q_ref[0, :, lanes] for lanes in head_lanes]

    def step(j, carry):
        rows = pl.ds(pl.multiple_of(j * tk, tk), tk)
        new = []
        for hh in range(2):
            m, acc = carry[hh]
            ks = k_ref[0, rows, head_lanes[hh]]
            vs = v_ref[0, rows, head_lanes[hh]]
            s = lax.dot_general(qs[hh], ks, (((1,), (1,)), ((), ())), preferred_element_type=F32)
            m_new = jnp.maximum(m, jnp.max(s, axis=-1, keepdims=True))
            p = jnp.exp2(s - m_new)
            acc = jnp.exp2(m - m_new) * acc + _dot(p.astype(BF16), vs)
            new.append((m_new, acc))
        return tuple(new)

    init = (jnp.full((tq, 1), -jnp.inf, F32), jnp.zeros((tq, LANES), F32))
    res = lax.fori_loop(0, nk, step, (init, init))
    lane = lax.broadcasted_iota(jnp.int32, (tq, LANES), 1)
    out = jnp.zeros((tq, LANES), F32)
    for hh in range(2):
        acc = res[hh][1]
        ones_lane = ONES_LANE[hh]
        is_v = (lane < V_DIM) if hh == 0 else (lane >= V_DIM)
        out = out + jnp.where(is_v, acc / acc[:, ones_lane:ones_lane + 1], 0.0)
    o_ref[0] = out


def _attention(q, k, v):
    b, s, _ = q.shape
    tq = min(ATT_TQ, s)
    tk = min(ATT_TK, s)
    pairs = ATT_HEADS // 2
    return pl.pallas_call(
        functools.partial(_attn_body, tk=tk, nk=s // tk),
        grid=(b, pairs, s // tq),
        in_specs=[pl.BlockSpec((1, tq, 2 * HEAD_PAD), lambda bi, hp, qi: (bi, qi, hp)),
                  pl.BlockSpec((1, s, 2 * HEAD_PAD), lambda bi, hp, qi: (bi, 0, hp),
                               pipeline_mode=pl.Buffered(1)),
                  pl.BlockSpec((1, s, 2 * HEAD_PAD), lambda bi, hp, qi: (bi, 0, hp),
                               pipeline_mode=pl.Buffered(1))],
        out_specs=pl.BlockSpec((1, tq, LANES), lambda bi, hp, qi: (bi, qi, hp)),
        out_shape=jax.ShapeDtypeStruct((b, s, ATT_WIDTH), F32),
        compiler_params=_params("arbitrary", "arbitrary", "arbitrary"),
        name="attention",
    )(q, k, v)


EXP_NEG_HALF = math.exp(-0.5)


def _rwkv_pre_body(z_ref, zp_ref, zn_ref, mu_ref, w0_ref, wup_ref, a0_ref, aup_ref, gup_ref,
                   kk_ref, ka_ref, rk_ref, bd_ref,
                   r_out, v_out, kk_out, wf_out, wb_out, kdf_out, kdb_out, bf_out, bb_out, g_out, bonus_out):
    i = pl.program_id(1)
    n = pl.num_programs(1)
    z = z_ref[0]
    ts = z.shape[0]
    prev_row = jnp.where(i > 0, zp_ref[0, SUBLANES - 1:SUBLANES, :], 0.0)
    next_row = jnp.where(i < n - 1, zn_ref[0, 0:1, :], 0.0)
    rows = lax.broadcasted_iota(jnp.int32, z.shape, 0)
    z_prev = jnp.where(rows == 0, prev_row, pltpu.roll(z, 1, 0))
    z_next = jnp.where(rows == ts - 1, next_row, pltpu.roll(z, ts - 1, 0))
    zs = z + mu_ref[0:1, :] * (z_prev - z) + mu_ref[1:2, :] * (z_next - z)

    w = RWKV_WIDTH
    r = zs[:, 0:w]
    k = zs[:, w:2 * w]
    v = zs[:, 2 * w:3 * w]
    wd = zs[:, 3 * w:3 * w + LANES]
    ad = zs[:, 3 * w + LANES:3 * w + 2 * LANES]
    gd = zs[:, 3 * w + 2 * LANES:]

    lora_w = _dot(jnp.tanh(wd).astype(BF16), wup_ref[...])
    lora_a = _dot(ad.astype(BF16), aup_ref[...])
    g_out[0] = _dot(jax.nn.sigmoid(gd).astype(BF16), gup_ref[...])

    ones_bd = bd_ref[...]
    kkf = k * kk_ref[...]
    kk = kkf / jnp.maximum(jnp.sqrt(_segsum(kkf * kkf, ones_bd)), 1e-12)
    r_out[0] = r
    v_out[0] = v
    kk_out[0] = kk

    k_sum = jnp.zeros_like(k)
    for d, (w_out, kd_out, b_out) in enumerate(((wf_out, kdf_out, bf_out), (wb_out, kdb_out, bb_out))):
        x = w0_ref[d:d + 1, :] + lora_w[:, d * w:(d + 1) * w]
        w_out[0] = jnp.exp(-EXP_NEG_HALF * jax.nn.sigmoid(x))
        a = jax.nn.sigmoid(a0_ref[d:d + 1, :] + lora_a[:, d * w:(d + 1) * w])
        k_d = k * (1.0 + (a - 1.0) * ka_ref[...])
        kd_out[0] = k_d
        b_out[0] = kk * a
        k_sum = k_sum + k_d
    bonus_out[0] = _segsum(r * k_sum * rk_ref[...], ones_bd) * v


def _rwkv_pre(zr, shift_mu, w0, wup_bd, a0, aup_bd, g_up, k_k, k_a, r_k, ones_bd):
    b, s, _ = zr.shape
    ts = min(PRE_TILE, s)
    halo = ts // SUBLANES
    last = s // SUBLANES - 1
    wide = lambda: pl.BlockSpec((1, ts, RWKV_WIDTH), lambda bi, i: (bi, i, 0))
    out = jax.ShapeDtypeStruct((b, s, RWKV_WIDTH), F32)
    return pl.pallas_call(
        _rwkv_pre_body,
        grid=(b, s // ts),
        in_specs=[pl.BlockSpec((1, ts, RWKV_COLS), lambda bi, i: (bi, i, 0)),
                  pl.BlockSpec((1, SUBLANES, RWKV_COLS), lambda bi, i: (bi, jnp.maximum(i * halo - 1, 0), 0)),
                  pl.BlockSpec((1, SUBLANES, RWKV_COLS), lambda bi, i: (bi, jnp.minimum((i + 1) * halo, last), 0)),
                  _const_spec((2, RWKV_COLS)),
                  _const_spec((2, RWKV_WIDTH)), _const_spec((LANES, 2 * RWKV_WIDTH)),
                  _const_spec((2, RWKV_WIDTH)), _const_spec((LANES, 2 * RWKV_WIDTH)),
                  _const_spec((GATE_LORA, RWKV_WIDTH)),
                  _const_spec((1, RWKV_WIDTH)), _const_spec((1, RWKV_WIDTH)), _const_spec((1, RWKV_WIDTH)),
                  _const_spec((RWKV_WIDTH, RWKV_WIDTH))],
        out_specs=[wide() for _ in range(11)],
        out_shape=[out] * 11,
        compiler_params=_params("arbitrary", "arbitrary"),
        name="rwkv_pre",
    )(zr, zr, zr, shift_mu, w0, wup_bd, a0, aup_bd, g_up, k_k, k_a, r_k, ones_bd)


HEAD_PAIRS = RWKV_HEADS // 2
N_CHAINS = 2 * HEAD_PAIRS
STATE_VREGS = RWKV_HEAD_DIM // SUBLANES


def _scan_body(*refs, tb, long_stream):
    n_streams = len(long_stream)
    ins = refs[:12 * n_streams]
    w2_ref, wo_ref = refs[12 * n_streams:12 * n_streams + 2]
    outs = refs[12 * n_streams + 2:14 * n_streams + 2]
    state_ref = refs[-1]

    for si, is_long in enumerate(long_stream):
        first = pl.program_id(1) == 0
        if is_long:
            first = first & (pl.program_id(0) == 0)

        @pl.when(first)
        def _(si=si):
            state_ref[si * N_CHAINS:(si + 1) * N_CHAINS] = jnp.zeros((N_CHAINS,) + state_ref.shape[1:], F32)

    w2 = w2_ref[...]
    wo = wo_ref[...]
    shape3 = (STATE_VREGS, SUBLANES, LANES)
    vrow = lax.broadcasted_iota(jnp.int32, shape3, 0) * SUBLANES + lax.broadcasted_iota(jnp.int32, shape3, 1)
    diag = vrow == lax.broadcasted_iota(jnp.int32, shape3, 2) % RWKV_HEAD_DIM

    n = RWKV_HEAD_DIM

    def group_sums(xs):
        parts = []
        for x3 in xs:
            hi, mid = _split_bf16(x3.reshape(n, LANES))
            parts.append(jnp.concatenate([hi, mid], axis=1))
        res = _dot(jnp.concatenate(parts, axis=0), w2)
        return [res[i * n:(i + 1) * n].reshape(shape3) for i in range(len(xs))]

    def paired_sums(xs):
        xb = [x3.reshape(n, LANES).astype(BF16) for x3 in xs]
        lhs = jnp.concatenate([jnp.concatenate(xb[i:i + 2], axis=1) for i in range(0, len(xb), 2)], axis=0)
        res = _dot(lhs, wo)
        return [res[(i // 2) * n:(i // 2 + 1) * n, (i % 2) * LANES:(i % 2 + 1) * LANES].reshape(shape3)
                for i in range(len(xs))]

    in_refs = [[ins[12 * si + 6 * d:12 * si + 6 * d + 6] for d in range(2)] for si in range(n_streams)]
    out_refs = [outs[2 * si:2 * si + 2] for si in range(n_streams)]

    n_groups = tb // SUBLANES
    sub = lax.broadcasted_iota(jnp.int32, (SUBLANES, LANES), 0)

    chains = [(si, d, hp) for si in range(n_streams) for d in range(2) for hp in range(HEAD_PAIRS)]
    n_chains = len(chains)
    groups = [range(si * N_CHAINS, (si + 1) * N_CHAINS) for si in range(n_streams)]

    def token_group(gi, carry):
        rows8 = [pl.ds(pl.multiple_of(g * SUBLANES, SUBLANES), SUBLANES) for g in (gi, n_groups - 1 - gi)]
        tiles = [[ref[0, rows8[d], hp * LANES:(hp + 1) * LANES] for ref in in_refs[si][d]] for si, d, hp in chains]
        states = [state_ref[c] for c in range(n_chains)]
        y_tiles = [jnp.zeros((SUBLANES, LANES), F32)] * n_chains
        for jj in range(SUBLANES):
            rows = []
            for c, (si, d, hp) in enumerate(chains):
                j = jj if d == 0 else SUBLANES - 1 - jj
                rows.append([t[j:j + 1, :][None] for t in tiles[c]])
            sa, v_col = [], []
            for cs in groups:
                sa += group_sums([states[c] * rows[c][0] for c in cs])
            for cs in groups:
                v_col += paired_sums([jnp.where(diag, rows[c][5], 0.0) for c in cs])
            for c in range(n_chains):
                _, w, b, kd, _, _ = rows[c]
                states[c] = states[c] * w - sa[c] * b + v_col[c] * kd
            for cs in groups:
                o = paired_sums([states[c] * rows[c][4] for c in cs])
                for i, c in enumerate(cs):
                    d = chains[c][1]
                    j = jj if d == 0 else SUBLANES - 1 - jj
                    y = jnp.sum(jnp.where(diag, o[i], 0.0).reshape(RWKV_HEAD_DIM, LANES), axis=0, keepdims=True)
                    y_tiles[c] = jnp.where(sub == j, y, y_tiles[c])
        for c, (si, d, hp) in enumerate(chains):
            state_ref[c] = states[c]
            out_refs[si][d][0, rows8[d], hp * LANES:(hp + 1) * LANES] = y_tiles[c]
        return carry

    lax.fori_loop(0, n_groups, token_group, 0)


def _rwkv_scan(streams, segments, w2, wo):
    seg_len = min(a[0].shape[1] for a, _ in streams)
    tb = min(SCAN_TB, seg_len)
    nb = seg_len // tb
    total = segments * nb
    in_specs, operands, out_specs, out_shapes, long_stream = [], [], [], [], []
    for arrays, off in streams:
        r, v, kk, w_f, w_b, kd_f, kd_b, b_f, b_b = arrays
        if off is None:
            assert r.shape[0] == 1 and r.shape[1] == segments * seg_len
            fwd_map = lambda h, j: (0, h * nb + j, 0)
            bwd_map = lambda h, j: (0, total - 1 - (h * nb + j), 0)
            out_shape = r.shape
        else:
            assert r.shape[1] == seg_len and off + segments <= r.shape[0]
            fwd_map = lambda h, j, off=off: (off + h, j, 0)
            bwd_map = lambda h, j, off=off: (off + h, nb - 1 - j, 0)
            out_shape = (segments, seg_len, RWKV_WIDTH)
        fwd = pl.BlockSpec((1, tb, RWKV_WIDTH), fwd_map)
        bwd = pl.BlockSpec((1, tb, RWKV_WIDTH), bwd_map)
        in_specs += [fwd] * 6 + [bwd] * 6
        operands += [kk, w_f, b_f, kd_f, r, v, kk, w_b, b_b, kd_b, r, v]
        out_fwd = fwd if off is None else pl.BlockSpec((1, tb, RWKV_WIDTH), lambda h, j: (h, j, 0))
        out_bwd = bwd if off is None else pl.BlockSpec((1, tb, RWKV_WIDTH), lambda h, j: (h, nb - 1 - j, 0))
        out_specs += [out_fwd, out_bwd]
        out_shapes += [jax.ShapeDtypeStruct(out_shape, F32)] * 2
        long_stream.append(off is None)
    n_streams = len(streams)
    return pl.pallas_call(
        functools.partial(_scan_body, tb=tb, long_stream=tuple(long_stream)),
        grid=(segments, nb),
        in_specs=in_specs + [_const_spec((2 * LANES, LANES)), _const_spec((2 * LANES, 2 * LANES))],
        out_specs=out_specs,
        out_shape=out_shapes,
        scratch_shapes=[pltpu.VMEM((n_streams * N_CHAINS, STATE_VREGS, SUBLANES, LANES), F32)],
        compiler_params=_params("arbitrary", "arbitrary"),
        name="rwkv_scan",
    )(*operands, w2, wo)


def _mix_out_body(att_ref, yf_ref, yb_ref, bonus_ref, g_ref, h_ref, an_ref, lnw_ref, lnb_ref, bd_ref,
                  wo_att_ref, wo_tm_ref, o_ref):
    att = _rms(att_ref[...], an_ref[...])
    ones_bd = bd_ref[...]
    y = yf_ref[...] + yb_ref[...]
    inv_n = 1.0 / RWKV_HEAD_DIM
    c = y - _segsum(y, ones_bd) * inv_n
    var = _segsum(c * c, ones_bd) * inv_n
    yn = c * lax.rsqrt(var + LN_X_EPS) * lnw_ref[...] + lnb_ref[...]
    tm = (yn + bonus_ref[...]) * g_ref[...]
    o_ref[...] = (h_ref[...] + _dot(att.astype(BF16), wo_att_ref[...])
                  + _dot(tm.astype(BF16), wo_tm_ref[...]))


def _mix_out(att, yf, yb, bonus, g, h, attn_norm, ln_w, ln_b, ones_bd, wo_att, wo_tm):
    t = h.shape[0]
    tm = min(ROW_TILE, t)
    half = pl.BlockSpec((tm, RWKV_WIDTH), lambda i: (i, 0))
    full = pl.BlockSpec((tm, D_MODEL), lambda i: (i, 0))
    vec = _const_spec((1, RWKV_WIDTH))
    return pl.pallas_call(
        _mix_out_body,
        grid=(t // tm,),
        in_specs=[half] * 5 + [full, vec, vec, vec, _const_spec((RWKV_WIDTH, RWKV_WIDTH)),
                               _const_spec((ATT_WIDTH, D_MODEL)), _const_spec((RWKV_WIDTH, D_MODEL))],
        out_specs=full,
        out_shape=jax.ShapeDtypeStruct((t, D_MODEL), F32),
        compiler_params=_params("arbitrary"),
        name="mix_out",
    )(att, yf, yb, bonus, g, h, attn_norm, ln_w, ln_b, ones_bd, wo_att, wo_tm)


def _rope_tables(seq):
    inv = 1.0 / (ROPE_THETA ** (jnp.arange(0, ROPE_DIM, 2, dtype=F32) / ROPE_DIM))
    ang = jnp.arange(seq, dtype=F32)[:, None] * inv[None, :]
    cos, sin = jnp.cos(ang), jnp.sin(ang)
    pad = jnp.zeros((seq, HEAD_PAD - QK_DIM), F32)
    cos_t = jnp.concatenate([jnp.ones((seq, NOPE_DIM), F32), cos, cos, pad], axis=1)
    sin_t = jnp.concatenate([jnp.zeros((seq, NOPE_DIM), F32), sin, sin, pad], axis=1)
    return cos_t, sin_t


def _rotate_half_cols(w):
    half = ROPE_DIM // 2
    return jnp.concatenate([-w[..., half:], w[..., :half]], axis=-1)


def _pad_head_cols(nope, rope):
    rows = nope.shape[0]
    pad = jnp.zeros((rows, ATT_HEADS, HEAD_PAD - QK_DIM), nope.dtype)
    return jnp.concatenate([nope, rope, pad], axis=-1).reshape(rows, QK_WIDTH)


def _block_diag2(a, b):
    za = jnp.zeros_like(a)
    return jnp.concatenate([jnp.concatenate([a, za], axis=1), jnp.concatenate([za, b], axis=1)], axis=0)


def _prepare(p):
    row = lambda a: a.reshape(1, -1)
    ffn_w = lambda wg, wu, wd: (
        wg.reshape(D_MODEL, N_FF_CHUNKS, FF_CHUNK).transpose(1, 0, 2).astype(BF16),
        wu.reshape(D_MODEL, N_FF_CHUNKS, FF_CHUNK).transpose(1, 0, 2).astype(BF16),
        wd.reshape(N_FF_CHUNKS, FF_CHUNK, D_MODEL).astype(BF16))

    w_in = p['w_in']
    off_ckv = Q_RANK
    off_kr = off_ckv + KV_RANK
    off_rwkv = off_kr + ROPE_DIM
    w_kr = w_in[:, off_kr:off_rwkv]
    kr_pad = lambda w: jnp.concatenate(
        [jnp.zeros((D_MODEL, NOPE_DIM), F32), w, jnp.zeros((D_MODEL, HEAD_PAD - QK_DIM), F32)], axis=1)
    w_in_arr = jnp.concatenate(
        [w_in[:, :off_kr], kr_pad(w_kr), kr_pad(_rotate_half_cols(w_kr)), w_in[:, off_rwkv:]], axis=1).astype(BF16)

    w_uq = p['w_uq'].reshape(Q_RANK, ATT_HEADS, QK_DIM)
    q_nope, q_rope = w_uq[..., :NOPE_DIM], w_uq[..., NOPE_DIM:]
    w_uq2 = jnp.concatenate(
        [_pad_head_cols(q_nope, q_rope), _pad_head_cols(jnp.zeros_like(q_nope), _rotate_half_cols(q_rope))],
        axis=1).astype(BF16)

    w_ukv = p['w_ukv'].reshape(KV_RANK, ATT_HEADS, NOPE_DIM + V_DIM)
    k_nope, v_w = w_ukv[..., :NOPE_DIM], w_ukv[..., NOPE_DIM:]
    w_k = _pad_head_cols(k_nope, jnp.zeros((KV_RANK, ATT_HEADS, ROPE_DIM), F32))
    zero_v = jnp.zeros((KV_RANK, ATT_HEADS // 2, V_DIM), F32)
    w_v = jnp.stack([jnp.concatenate([v_w[:, 0::2], zero_v], axis=-1),
                     jnp.concatenate([zero_v, v_w[:, 1::2]], axis=-1)], axis=2).reshape(KV_RANK, QK_WIDTH)
    w_kv2 = jnp.concatenate([w_k, w_v], axis=1).astype(BF16)

    head = jnp.arange(RWKV_WIDTH) // RWKV_HEAD_DIM
    ones_bd = (head[:, None] == head[None, :]).astype(BF16)
    pair = jnp.arange(LANES) // RWKV_HEAD_DIM
    ones_pair = (pair[:, None] == pair[None, :]).astype(BF16)

    return dict(
        ffn1=(row(p['ffn1_norm']),) + ffn_w(p['ffn1_w_gate'], p['ffn1_w_up'], p['ffn1_w_down']),
        ffn2=(row(p['ffn2_norm']),) + ffn_w(p['ffn2_w_gate'], p['ffn2_w_up'], p['ffn2_w_down']),
        inproj=(row(p['mix_norm']), w_in_arr, row(p['q_norm']), w_uq2, row(p['kv_norm']), w_kv2),
        pre=(p['shift_mu'], p['w0'], _block_diag2(p['w_up'][0], p['w_up'][1]).astype(BF16),
             p['a0'], _block_diag2(p['a_up'][0], p['a_up'][1]).astype(BF16), p['g_up'].astype(BF16),
             row(p['k_k']), row(p['k_a']), row(p['r_k']), ones_bd),
        scan=(jnp.concatenate([ones_pair, ones_pair], axis=0), _block_diag2(ones_pair, ones_pair)),
        out=(row(p['attn_out_norm']), row(p['ln_x_w']), row(p['ln_x_b']), ones_bd,
             p['w_out'][:ATT_WIDTH].astype(BF16), p['w_out'][ATT_WIDTH:].astype(BF16)),
    )


def _before_scan(x, w, final_g):
    b, s, _ = x.shape
    h = _ffn(x.reshape(b * s, D_MODEL), *w['ffn1'], final_g, final=False)
    cos_t, sin_t = _rope_tables(s)
    q, k, v, zr = _inproj(h, s, *w['inproj'], cos_t, sin_t)
    att = _attention(q.reshape(b, s, QK_WIDTH), k.reshape(b, s, QK_WIDTH), v.reshape(b, s, QK_WIDTH))
    *scan_in, g, bonus = _rwkv_pre(zr.reshape(b, s, RWKV_COLS), *w['pre'])
    return h, att, g, bonus, tuple(scan_in)


def _after_scan(shape, h, att, g, bonus, y_f, y_b, w, final_g):
    flat = lambda a: a.reshape(-1, RWKV_WIDTH)
    h = _mix_out(flat(att), flat(y_f), flat(y_b), flat(bonus), flat(g), h, *w['out'])
    return _ffn(h, *w['ffn2'], final_g, final=True).reshape(shape)


def _scan_all(scan_p, scan_s, w):
    bp, sp, _ = scan_p[0].shape
    bs, ss, _ = scan_s[0].shape
    segments = ss // sp
    if bs == 1 and ss == segments * sp and bp % segments == 0:
        offsets = list(range(0, bp, segments))
        outs = _rwkv_scan([(scan_p, off) for off in offsets] + [(scan_s, None)], segments, *w['scan'])
        y_p = [jnp.concatenate(outs[d:2 * len(offsets):2], axis=0) for d in range(2)]
        return y_p, outs[2 * len(offsets):]
    outs_p = _rwkv_scan([(scan_p, off) for off in range(bp)], 1, *w['scan'])
    y_p = [jnp.concatenate(outs_p[d::2], axis=0) for d in range(2)]
    outs_s = _rwkv_scan([(tuple(a[i:i + 1] for a in scan_s), None) for i in range(bs)], 1, *w['scan'])
    return y_p, [jnp.concatenate(outs_s[d::2], axis=0) for d in range(2)]


def kernel(x_prompt, x_sample, ffn1_norm, ffn1_w_gate, ffn1_w_up, ffn1_w_down, mix_norm, w_in, q_norm, w_uq,
           kv_norm, w_ukv, attn_out_norm, shift_mu, w0, w_up, a0, a_up, g_up, k_k, k_a, r_k, ln_x_w, ln_x_b,
           w_out, ffn2_norm, ffn2_w_gate, ffn2_w_up, ffn2_w_down, final_norm):
    assert ffn1_norm.shape[0] == 1, "single-layer trunk"
    p = dict(ffn1_norm=ffn1_norm[0], ffn1_w_gate=ffn1_w_gate[0], ffn1_w_up=ffn1_w_up[0],
             ffn1_w_down=ffn1_w_down[0], mix_norm=mix_norm[0], w_in=w_in[0], q_norm=q_norm[0], w_uq=w_uq[0],
             kv_norm=kv_norm[0], w_ukv=w_ukv[0], attn_out_norm=attn_out_norm[0], shift_mu=shift_mu[0],
             w0=w0[0], w_up=w_up[0], a0=a0[0], a_up=a_up[0], g_up=g_up[0], k_k=k_k[0], k_a=k_a[0],
             r_k=r_k[0], ln_x_w=ln_x_w[0], ln_x_b=ln_x_b[0], w_out=w_out[0], ffn2_norm=ffn2_norm[0],
             ffn2_w_gate=ffn2_w_gate[0], ffn2_w_up=ffn2_w_up[0], ffn2_w_down=ffn2_w_down[0])
    w = _prepare(p)
    final_g = final_norm.reshape(1, D_MODEL)
    *pre_p, scan_p = _before_scan(x_prompt, w, final_g)
    *pre_s, scan_s = _before_scan(x_sample, w, final_g)
    y_p, y_s = _scan_all(scan_p, scan_s, w)
    return (_after_scan(x_prompt.shape, *pre_p, *y_p, w, final_g),
            _after_scan(x_sample.shape, *pre_s, *y_s, w, final_g))
```

```python
import functools
import math

import jax
import jax.numpy as jnp
from jax import lax
from jax.experimental import pallas as pl
from jax.experimental.pallas import tpu as pltpu

F32 = jnp.float32
BF16 = jnp.bfloat16

D_MODEL = 1024
ATT_HEADS = 8
NOPE_DIM = 64
ROPE_DIM = 32
QK_DIM = NOPE_DIM + ROPE_DIM
V_DIM = 64
Q_RANK = 384
KV_RANK = 256
ATT_WIDTH = ATT_HEADS * V_DIM
ROPE_THETA = 10000.0
RWKV_HEADS = 8
RWKV_HEAD_DIM = 64
RWKV_WIDTH = RWKV_HEADS * RWKV_HEAD_DIM
DECAY_LORA = 64
ICLR_LORA = 64
GATE_LORA = 128
D_FF = 2816
NORM_EPS = 1e-6
LN_X_EPS = 64e-5
RWKV_COLS = 3 * RWKV_WIDTH + 2 * DECAY_LORA + 2 * ICLR_LORA + GATE_LORA

LANES = 128
SUBLANES = 8
HEAD_PAD = LANES
QK_WIDTH = ATT_HEADS * HEAD_PAD
FF_CHUNK = 256
N_FF_CHUNKS = D_FF // FF_CHUNK
VMEM_LIMIT = 56 * 1024 * 1024

ROW_TILE = 512
PRE_TILE = 256
ATT_TQ = 512
ATT_TK = 1024
ONES_LANE = (V_DIM, 0)
SCAN_TB = 128


def _rms(x, g):
    return x * lax.rsqrt(jnp.mean(x * x, axis=-1, keepdims=True) + NORM_EPS) * g


def _dot(a, b):
    return jnp.dot(a, b, preferred_element_type=F32)


def _split_bf16(x):
    hi = x.astype(BF16)
    mid = (x - hi.astype(F32)).astype(BF16)
    return hi, mid


def _segsum(x, ones_bd):
    hi, mid = _split_bf16(x)
    return _dot(hi, ones_bd) + _dot(mid, ones_bd)


def _const_spec(shape):
    return pl.BlockSpec(shape, lambda *_: (0,) * len(shape), pipeline_mode=pl.Buffered(1))


def _params(*sem):
    return pltpu.CompilerParams(dimension_semantics=sem, vmem_limit_bytes=VMEM_LIMIT)


def _ffn_body(x_ref, g_ref, wg_ref, wu_ref, wd_ref, fg_ref, o_ref, acc_ref, *, final):
    x = x_ref[...]
    u = _rms(x, g_ref[...]).astype(BF16)
    acc_ref[...] = jnp.zeros_like(acc_ref)

    def chunk(j, carry):
        gate = _dot(u, wg_ref[j])
        up = _dot(u, wu_ref[j])
        act = (gate * jax.nn.sigmoid(gate) * up).astype(BF16)
        acc_ref[...] += _dot(act, wd_ref[j])
        return carry

    lax.fori_loop(0, N_FF_CHUNKS, chunk, 0)
    h = x + 0.5 * acc_ref[...]
    if final:
        h = _rms(h, fg_ref[...])
    o_ref[...] = h


def _ffn(x, norm_g, wg, wu, wd, final_g, *, final):
    t = x.shape[0]
    tm = min(ROW_TILE, t)
    row = pl.BlockSpec((tm, D_MODEL), lambda i: (i, 0))
    return pl.pallas_call(
        functools.partial(_ffn_body, final=final),
        grid=(t // tm,),
        in_specs=[row, _const_spec((1, D_MODEL)),
                  _const_spec((N_FF_CHUNKS, D_MODEL, FF_CHUNK)),
                  _const_spec((N_FF_CHUNKS, D_MODEL, FF_CHUNK)),
                  _const_spec((N_FF_CHUNKS, FF_CHUNK, D_MODEL)),
                  _const_spec((1, D_MODEL))],
        out_specs=row,
        out_shape=jax.ShapeDtypeStruct((t, D_MODEL), F32),
        scratch_shapes=[pltpu.VMEM((tm, D_MODEL), F32)],
        compiler_params=_params("arbitrary"),
        name="ffn_final" if final else "ffn",
    )(x, norm_g, wg, wu, wd, final_g)


IN_ARR_COLS = Q_RANK + KV_RANK + 2 * LANES + RWKV_COLS
OFF_KR_ARR = Q_RANK + KV_RANK
OFF_RWKV_ARR = OFF_KR_ARR + 2 * LANES


def _inproj_body(h_ref, g_ref, win_ref, qn_ref, wuq_ref, kvn_ref, wkv_ref, cos_ref, sin_ref,
                 q_ref, k_ref, v_ref, zr_ref):
    u = _rms(h_ref[...], g_ref[...]).astype(BF16)
    z = _dot(u, win_ref[...])
    zr_ref[...] = z[:, OFF_RWKV_ARR:]

    cos = cos_ref[...]
    sin = sin_ref[...]
    cos_h = jnp.concatenate([cos] * ATT_HEADS, axis=1)
    sin_h = jnp.concatenate([sin] * ATT_HEADS, axis=1)

    q2 = _dot(_rms(z[:, :Q_RANK], qn_ref[...]).astype(BF16), wuq_ref[...])
    q = (q2[:, :QK_WIDTH] * cos_h + q2[:, QK_WIDTH:] * sin_h) * (QK_DIM ** -0.5 * math.log2(math.e))
    q_ref[...] = q.astype(BF16)

    kv = _dot(_rms(z[:, Q_RANK:OFF_KR_ARR], kvn_ref[...]).astype(BF16), wkv_ref[...])
    k_pe = z[:, OFF_KR_ARR:OFF_KR_ARR + LANES] * cos + z[:, OFF_KR_ARR + LANES:OFF_RWKV_ARR] * sin
    k = kv[:, :QK_WIDTH] + jnp.concatenate([k_pe] * ATT_HEADS, axis=1)
    k_ref[...] = k.astype(BF16)
    v = kv[:, QK_WIDTH:]
    col = lax.broadcasted_iota(jnp.int32, v.shape, 1) % (2 * HEAD_PAD)
    is_ones = (col == ONES_LANE[0]) | (col == HEAD_PAD + ONES_LANE[1])
    v_ref[...] = jnp.where(is_ones, 1.0, v).astype(BF16)


def _inproj(h, seq, mix_g, w_in_arr, q_norm, w_uq2, kv_norm, w_kv2, cos_t, sin_t):
    t = h.shape[0]
    tm = min(ROW_TILE, seq)
    per_seq = seq // tm
    row = lambda w: pl.BlockSpec((tm, w), lambda i: (i, 0))
    pos = pl.BlockSpec((tm, LANES), lambda i: (i % per_seq, 0))
    return pl.pallas_call(
        _inproj_body,
        grid=(t // tm,),
        in_specs=[row(D_MODEL), _const_spec((1, D_MODEL)), _const_spec((D_MODEL, IN_ARR_COLS)),
                  _const_spec((1, Q_RANK)), _const_spec((Q_RANK, 2 * QK_WIDTH)),
                  _const_spec((1, KV_RANK)), _const_spec((KV_RANK, 2 * QK_WIDTH)),
                  pos, pos],
        out_specs=[row(QK_WIDTH), row(QK_WIDTH), row(QK_WIDTH), row(RWKV_COLS)],
        out_shape=[jax.ShapeDtypeStruct((t, QK_WIDTH), BF16),
                   jax.ShapeDtypeStruct((t, QK_WIDTH), BF16),
                   jax.ShapeDtypeStruct((t, QK_WIDTH), BF16),
                   jax.ShapeDtypeStruct((t, RWKV_COLS), F32)],
        compiler_params=_params("arbitrary"),
        name="inproj",
    )(h, mix_g, w_in_arr, q_norm, w_uq2, kv_norm, w_kv2, cos_t, sin_t)


def _attn_body(q_ref, k_ref, v_ref, o_ref, *, tk, nk):
    tq = q_ref.shape[1]
    head_lanes = [slice(hh * HEAD_PAD, (hh + 1) * HEAD_PAD) for hh in range(2)]
    qs = [q_ref[0, :, lanes] for lanes in head_lanes]

    def step(j, carry):
        rows = pl.ds(pl.multiple_of(j * tk, tk), tk)
        new = []
        scores = [lax.dot_general(qs[hh], k_ref[0, rows, head_lanes[hh]], (((1,), (1,)), ((), ())),
                                  preferred_element_type=F32) for hh in range(2)]
        for hh in range(2):
            m, acc = carry[hh]
            vs = v_ref[0, rows, head_lanes[hh]]
            s = scores[hh]
            m_new = jnp.maximum(m, jnp.max(s, axis=-1, keepdims=True))
            p = jnp.exp2(s - m_new)
            acc = jnp.exp2(m - m_new) * acc + _dot(p.astype(BF16), vs)
            new.append((m_new, acc))
        return tuple(new)

    init = (jnp.full((tq, 1), -jnp.inf, F32), jnp.zeros((tq, LANES), F32))
    res = lax.fori_loop(0, nk, step, (init, init))
    lane = lax.broadcasted_iota(jnp.int32, (tq, LANES), 1)
    out = jnp.zeros((tq, LANES), F32)
    for hh in range(2):
        acc = res[hh][1]
        ones_lane = ONES_LANE[hh]
        is_v = (lane < V_DIM) if hh == 0 else (lane >= V_DIM)
        out = out + jnp.where(is_v, acc / acc[:, ones_lane:ones_lane + 1], 0.0)
    o_ref[0] = out


def _attention(q, k, v):
    b, s, _ = q.shape
    tq = min(ATT_TQ, s)
    tk = min(ATT_TK, s)
    pairs = ATT_HEADS // 2
    return pl.pallas_call(
        functools.partial(_attn_body, tk=tk, nk=s // tk),
        grid=(b, pairs, s // tq),
        in_specs=[pl.BlockSpec((1, tq, 2 * HEAD_PAD), lambda bi, hp, qi: (bi, qi, hp)),
                  pl.BlockSpec((1, s, 2 * HEAD_PAD), lambda bi, hp, qi: (bi, 0, hp),
                               pipeline_mode=pl.Buffered(1)),
                  pl.BlockSpec((1, s, 2 * HEAD_PAD), lambda bi, hp, qi: (bi, 0, hp),
                               pipeline_mode=pl.Buffered(1))],
        out_specs=pl.BlockSpec((1, tq, LANES), lambda bi, hp, qi: (bi, qi, hp)),
        out_shape=jax.ShapeDtypeStruct((b, s, ATT_WIDTH), F32),
        compiler_params=_params("arbitrary", "arbitrary", "arbitrary"),
        name="attention",
    )(q, k, v)


EXP_NEG_HALF = math.exp(-0.5)


def _rwkv_pre_body(z_ref, zp_ref, zn_ref, mu_ref, w0_ref, wup_ref, a0_ref, aup_ref, gup_ref,
                   kk_ref, ka_ref, rk_ref, bd_ref,
                   r_out, v_out, kk_out, wf_out, wb_out, kdf_out, kdb_out, bf_out, bb_out, g_out, bonus_out):
    i = pl.program_id(1)
    n = pl.num_programs(1)
    z = z_ref[0]
    ts = z.shape[0]
    prev_row = jnp.where(i > 0, zp_ref[0, SUBLANES - 1:SUBLANES, :], 0.0)
    next_row = jnp.where(i < n - 1, zn_ref[0, 0:1, :], 0.0)
    rows = lax.broadcasted_iota(jnp.int32, z.shape, 0)
    z_prev = jnp.where(rows == 0, prev_row, pltpu.roll(z, 1, 0))
    z_next = jnp.where(rows == ts - 1, next_row, pltpu.roll(z, ts - 1, 0))
    zs = z + mu_ref[0:1, :] * (z_prev - z) + mu_ref[1:2, :] * (z_next - z)

    w = RWKV_WIDTH
    r = zs[:, 0:w]
    k = zs[:, w:2 * w]
    v = zs[:, 2 * w:3 * w]
    wd = zs[:, 3 * w:3 * w + LANES]
    ad = zs[:, 3 * w + LANES:3 * w + 2 * LANES]
    gd = zs[:, 3 * w + 2 * LANES:]

    lora_w = _dot(jnp.tanh(wd).astype(BF16), wup_ref[...])
    lora_a = _dot(ad.astype(BF16), aup_ref[...])
    g_out[0] = _dot(jax.nn.sigmoid(gd).astype(BF16), gup_ref[...])

    ones_bd = bd_ref[...]
    kkf = k * kk_ref[...]
    kk = kkf / jnp.maximum(jnp.sqrt(_segsum(kkf * kkf, ones_bd)), 1e-12)
    r_out[0] = r
    v_out[0] = v
    kk_out[0] = kk

    k_sum = jnp.zeros_like(k)
    for d, (w_out, kd_out, b_out) in enumerate(((wf_out, kdf_out, bf_out), (wb_out, kdb_out, bb_out))):
        x = w0_ref[d:d + 1, :] + lora_w[:, d * w:(d + 1) * w]
        w_out[0] = jnp.exp(-EXP_NEG_HALF * jax.nn.sigmoid(x))
        a = jax.nn.sigmoid(a0_ref[d:d + 1, :] + lora_a[:, d * w:(d + 1) * w])
        k_d = k * (1.0 + (a - 1.0) * ka_ref[...])
        kd_out[0] = k_d
        b_out[0] = kk * a
        k_sum = k_sum + k_d
    bonus_out[0] = _segsum(r * k_sum * rk_ref[...], ones_bd) * v


def _rwkv_pre(zr, shift_mu, w0, wup_bd, a0, aup_bd, g_up, k_k, k_a, r_k, ones_bd):
    b, s, _ = zr.shape
    ts = min(PRE_TILE, s)
    halo = ts // SUBLANES
    last = s // SUBLANES - 1
    wide = lambda: pl.BlockSpec((1, ts, RWKV_WIDTH), lambda bi, i: (bi, i, 0))
    out = jax.ShapeDtypeStruct((b, s, RWKV_WIDTH), F32)
    return pl.pallas_call(
        _rwkv_pre_body,
        grid=(b, s // ts),
        in_specs=[pl.BlockSpec((1, ts, RWKV_COLS), lambda bi, i: (bi, i, 0)),
                  pl.BlockSpec((1, SUBLANES, RWKV_COLS), lambda bi, i: (bi, jnp.maximum(i * halo - 1, 0), 0)),
                  pl.BlockSpec((1, SUBLANES, RWKV_COLS), lambda bi, i: (bi, jnp.minimum((i + 1) * halo, last), 0)),
                  _const_spec((2, RWKV_COLS)),
                  _const_spec((2, RWKV_WIDTH)), _const_spec((LANES, 2 * RWKV_WIDTH)),
                  _const_spec((2, RWKV_WIDTH)), _const_spec((LANES, 2 * RWKV_WIDTH)),
                  _const_spec((GATE_LORA, RWKV_WIDTH)),
                  _const_spec((1, RWKV_WIDTH)), _const_spec((1, RWKV_WIDTH)), _const_spec((1, RWKV_WIDTH)),
                  _const_spec((RWKV_WIDTH, RWKV_WIDTH))],
        out_specs=[wide() for _ in range(11)],
        out_shape=[out] * 11,
        compiler_params=_params("arbitrary", "arbitrary"),
        name="rwkv_pre",
    )(zr, zr, zr, shift_mu, w0, wup_bd, a0, aup_bd, g_up, k_k, k_a, r_k, ones_bd)


HEAD_PAIRS = RWKV_HEADS // 2
N_CHAINS = 2 * HEAD_PAIRS
STATE_VREGS = RWKV_HEAD_DIM // SUBLANES


def _scan_body(*refs, tb, long_stream):
    n_streams = len(long_stream)
    ins = refs[:12 * n_streams]
    w2_ref, wo_ref = refs[12 * n_streams:12 * n_streams + 2]
    outs = refs[12 * n_streams + 2:14 * n_streams + 2]
    state_ref = refs[-1]

    for si, is_long in enumerate(long_stream):
        first = pl.program_id(1) == 0
        if is_long:
            first = first & (pl.program_id(0) == 0)

        @pl.when(first)
        def _(si=si):
            state_ref[si * N_CHAINS:(si + 1) * N_CHAINS] = jnp.zeros((N_CHAINS,) + state_ref.shape[1:], F32)

    w2 = w2_ref[...]
    wo = wo_ref[...]
    shape3 = (STATE_VREGS, SUBLANES, LANES)
    vrow = lax.broadcasted_iota(jnp.int32, shape3, 0) * SUBLANES + lax.broadcasted_iota(jnp.int32, shape3, 1)
    diag = vrow == lax.broadcasted_iota(jnp.int32, shape3, 2) % RWKV_HEAD_DIM

    n = RWKV_HEAD_DIM

    def group_sums(xs):
        parts = []
        for x3 in xs:
            hi, mid = _split_bf16(x3.reshape(n, LANES))
            parts.append(jnp.concatenate([hi, mid], axis=1))
        res = _dot(jnp.concatenate(parts, axis=0), w2)
        return [res[i * n:(i + 1) * n].reshape(shape3) for i in range(len(xs))]

    def paired_sums(xs):
        xb = [x3.reshape(n, LANES).astype(BF16) for x3 in xs]
        lhs = jnp.concatenate([jnp.concatenate(xb[i:i + 2], axis=1) for i in range(0, len(xb), 2)], axis=0)
        res = _dot(lhs, wo)
        return [res[(i // 2) * n:(i // 2 + 1) * n, (i % 2) * LANES:(i % 2 + 1) * LANES].reshape(shape3)
                for i in range(len(xs))]

    in_refs = [[ins[12 * si + 6 * d:12 * si + 6 * d + 6] for d in range(2)] for si in range(n_streams)]
    out_refs = [outs[2 * si:2 * si + 2] for si in range(n_streams)]

    n_groups = tb // SUBLANES
    sub = lax.broadcasted_iota(jnp.int32, (SUBLANES, LANES), 0)

    chains = [(si, d, hp) for si in range(n_streams) for d in range(2) for hp in range(HEAD_PAIRS)]
    n_chains = len(chains)
    groups = [range(si * N_CHAINS, (si + 1) * N_CHAINS) for si in range(n_streams)]

    def token_group(gi, carry):
        rows8 = [pl.ds(pl.multiple_of(g * SUBLANES, SUBLANES), SUBLANES) for g in (gi, n_groups - 1 - gi)]
        tiles = [[ref[0, rows8[d], hp * LANES:(hp + 1) * LANES] for ref in in_refs[si][d]] for si, d, hp in chains]
        states = [state_ref[c] for c in range(n_chains)]
        y_tiles = [jnp.zeros((SUBLANES, LANES), F32)] * n_chains
        for jj in range(SUBLANES):
            rows = []
            for c, (si, d, hp) in enumerate(chains):
                j = jj if d == 0 else SUBLANES - 1 - jj
                rows.append([t[j:j + 1, :][None] for t in tiles[c]])
            sa, v_col = [], []
            for cs in groups:
                sa += group_sums([states[c] * rows[c][0] for c in cs])
            for cs in groups:
                v_col += paired_sums([jnp.where(diag, rows[c][5], 0.0) for c in cs])
            for c in range(n_chains):
                _, w, b, kd, _, _ = rows[c]
                states[c] = states[c] * w - sa[c] * b + v_col[c] * kd
            for cs in groups:
                o = paired_sums([states[c] * rows[c][4] for c in cs])
                for i, c in enumerate(cs):
                    d = chains[c][1]
                    j = jj if d == 0 else SUBLANES - 1 - jj
                    y = jnp.sum(jnp.where(diag, o[i], 0.0).reshape(RWKV_HEAD_DIM, LANES), axis=0, keepdims=True)
                    y_tiles[c] = jnp.where(sub == j, y, y_tiles[c])
        for c, (si, d, hp) in enumerate(chains):
            state_ref[c] = states[c]
            out_refs[si][d][0, rows8[d], hp * LANES:(hp + 1) * LANES] = y_tiles[c]
        return carry

    lax.fori_loop(0, n_groups, token_group, 0)


def _rwkv_scan(streams, segments, w2, wo):
    seg_len = min(a[0].shape[1] for a, _ in streams)
    tb = min(SCAN_TB, seg_len)
    nb = seg_len // tb
    total = segments * nb
    in_specs, operands, out_specs, out_shapes, long_stream = [], [], [], [], []
    for arrays, off in streams:
        r, v, kk, w_f, w_b, kd_f, kd_b, b_f, b_b = arrays
        if off is None:
            assert r.shape[0] == 1 and r.shape[1] == segments * seg_len
            fwd_map = lambda h, j: (0, h * nb + j, 0)
            bwd_map = lambda h, j: (0, total - 1 - (h * nb + j), 0)
            out_shape = r.shape
        else:
            assert r.shape[1] == seg_len and off + segments <= r.shape[0]
            fwd_map = lambda h, j, off=off: (off + h, j, 0)
            bwd_map = lambda h, j, off=off: (off + h, nb - 1 - j, 0)
            out_shape = (segments, seg_len, RWKV_WIDTH)
        fwd = pl.BlockSpec((1, tb, RWKV_WIDTH), fwd_map)
        bwd = pl.BlockSpec((1, tb, RWKV_WIDTH), bwd_map)
        in_specs += [fwd] * 6 + [bwd] * 6
        operands += [kk, w_f, b_f, kd_f, r, v, kk, w_b, b_b, kd_b, r, v]
        out_fwd = fwd if off is None else pl.BlockSpec((1, tb, RWKV_WIDTH), lambda h, j: (h, j, 0))
        out_bwd = bwd if off is None else pl.BlockSpec((1, tb, RWKV_WIDTH), lambda h, j: (h, nb - 1 - j, 0))
        out_specs += [out_fwd, out_bwd]
        out_shapes += [jax.ShapeDtypeStruct(out_shape, F32)] * 2
        long_stream.append(off is None)
    n_streams = len(streams)
    return pl.pallas_call(
        functools.partial(_scan_body, tb=tb, long_stream=tuple(long_stream)),
        grid=(segments, nb),
        in_specs=in_specs + [_const_spec((2 * LANES, LANES)), _const_spec((2 * LANES, 2 * LANES))],
        out_specs=out_specs,
        out_shape=out_shapes,
        scratch_shapes=[pltpu.VMEM((n_streams * N_CHAINS, STATE_VREGS, SUBLANES, LANES), F32)],
        compiler_params=_params("arbitrary", "arbitrary"),
        name="rwkv_scan",
    )(*operands, w2, wo)


def _mix_out_body(att_ref, yf_ref, yb_ref, bonus_ref, g_ref, h_ref, an_ref, lnw_ref, lnb_ref, bd_ref,
                  wo_att_ref, wo_tm_ref, o_ref):
    att = _rms(att_ref[...], an_ref[...])
    ones_bd = bd_ref[...]
    y = yf_ref[...] + yb_ref[...]
    inv_n = 1.0 / RWKV_HEAD_DIM
    c = y - _segsum(y, ones_bd) * inv_n
    var = _segsum(c * c, ones_bd) * inv_n
    yn = c * lax.rsqrt(var + LN_X_EPS) * lnw_ref[...] + lnb_ref[...]
    tm = (yn + bonus_ref[...]) * g_ref[...]
    o_ref[...] = (h_ref[...] + _dot(att.astype(BF16), wo_att_ref[...])
                  + _dot(tm.astype(BF16), wo_tm_ref[...]))


def _mix_out(att, yf, yb, bonus, g, h, attn_norm, ln_w, ln_b, ones_bd, wo_att, wo_tm):
    t = h.shape[0]
    tm = min(ROW_TILE, t)
    half = pl.BlockSpec((tm, RWKV_WIDTH), lambda i: (i, 0))
    full = pl.BlockSpec((tm, D_MODEL), lambda i: (i, 0))
    vec = _const_spec((1, RWKV_WIDTH))
    return pl.pallas_call(
        _mix_out_body,
        grid=(t // tm,),
        in_specs=[half] * 5 + [full, vec, vec, vec, _const_spec((RWKV_WIDTH, RWKV_WIDTH)),
                               _const_spec((ATT_WIDTH, D_MODEL)), _const_spec((RWKV_WIDTH, D_MODEL))],
        out_specs=full,
        out_shape=jax.ShapeDtypeStruct((t, D_MODEL), F32),
        compiler_params=_params("arbitrary"),
        name="mix_out",
    )(att, yf, yb, bonus, g, h, attn_norm, ln_w, ln_b, ones_bd, wo_att, wo_tm)


def _rope_tables(seq):
    inv = 1.0 / (ROPE_THETA ** (jnp.arange(0, ROPE_DIM, 2, dtype=F32) / ROPE_DIM))
    ang = jnp.arange(seq, dtype=F32)[:, None] * inv[None, :]
    cos, sin = jnp.cos(ang), jnp.sin(ang)
    pad = jnp.zeros((seq, HEAD_PAD - QK_DIM), F32)
    cos_t = jnp.concatenate([jnp.ones((seq, NOPE_DIM), F32), cos, cos, pad], axis=1)
    sin_t = jnp.concatenate([jnp.zeros((seq, NOPE_DIM), F32), sin, sin, pad], axis=1)
    return cos_t, sin_t


def _rotate_half_cols(w):
    half = ROPE_DIM // 2
    return jnp.concatenate([-w[..., half:], w[..., :half]], axis=-1)


def _pad_head_cols(nope, rope):
    rows = nope.shape[0]
    pad = jnp.zeros((rows, ATT_HEADS, HEAD_PAD - QK_DIM), nope.dtype)
    return jnp.concatenate([nope, rope, pad], axis=-1).reshape(rows, QK_WIDTH)


def _block_diag2(a, b):
    za = jnp.zeros_like(a)
    return jnp.concatenate([jnp.concatenate([a, za], axis=1), jnp.concatenate([za, b], axis=1)], axis=0)


def _prepare(p):
    row = lambda a: a.reshape(1, -1)
    ffn_w = lambda wg, wu, wd: (
        wg.reshape(D_MODEL, N_FF_CHUNKS, FF_CHUNK).transpose(1, 0, 2).astype(BF16),
        wu.reshape(D_MODEL, N_FF_CHUNKS, FF_CHUNK).transpose(1, 0, 2).astype(BF16),
        wd.reshape(N_FF_CHUNKS, FF_CHUNK, D_MODEL).astype(BF16))

    w_in = p['w_in']
    off_ckv = Q_RANK
    off_kr = off_ckv + KV_RANK
    off_rwkv = off_kr + ROPE_DIM
    w_kr = w_in[:, off_kr:off_rwkv]
    kr_pad = lambda w: jnp.concatenate(
        [jnp.zeros((D_MODEL, NOPE_DIM), F32), w, jnp.zeros((D_MODEL, HEAD_PAD - QK_DIM), F32)], axis=1)
    w_in_arr = jnp.concatenate(
        [w_in[:, :off_kr], kr_pad(w_kr), kr_pad(_rotate_half_cols(w_kr)), w_in[:, off_rwkv:]], axis=1).astype(BF16)

    w_uq = p['w_uq'].reshape(Q_RANK, ATT_HEADS, QK_DIM)
    q_nope, q_rope = w_uq[..., :NOPE_DIM], w_uq[..., NOPE_DIM:]
    w_uq2 = jnp.concatenate(
        [_pad_head_cols(q_nope, q_rope), _pad_head_cols(jnp.zeros_like(q_nope), _rotate_half_cols(q_rope))],
        axis=1).astype(BF16)

    w_ukv = p['w_ukv'].reshape(KV_RANK, ATT_HEADS, NOPE_DIM + V_DIM)
    k_nope, v_w = w_ukv[..., :NOPE_DIM], w_ukv[..., NOPE_DIM:]
    w_k = _pad_head_cols(k_nope, jnp.zeros((KV_RANK, ATT_HEADS, ROPE_DIM), F32))
    zero_v = jnp.zeros((KV_RANK, ATT_HEADS // 2, V_DIM), F32)
    w_v = jnp.stack([jnp.concatenate([v_w[:, 0::2], zero_v], axis=-1),
                     jnp.concatenate([zero_v, v_w[:, 1::2]], axis=-1)], axis=2).reshape(KV_RANK, QK_WIDTH)
    w_kv2 = jnp.concatenate([w_k, w_v], axis=1).astype(BF16)

    head = jnp.arange(RWKV_WIDTH) // RWKV_HEAD_DIM
    ones_bd = (head[:, None] == head[None, :]).astype(BF16)
    pair = jnp.arange(LANES) // RWKV_HEAD_DIM
    ones_pair = (pair[:, None] == pair[None, :]).astype(BF16)

    return dict(
        ffn1=(row(p['ffn1_norm']),) + ffn_w(p['ffn1_w_gate'], p['ffn1_w_up'], p['ffn1_w_down']),
        ffn2=(row(p['ffn2_norm']),) + ffn_w(p['ffn2_w_gate'], p['ffn2_w_up'], p['ffn2_w_down']),
        inproj=(row(p['mix_norm']), w_in_arr, row(p['q_norm']), w_uq2, row(p['kv_norm']), w_kv2),
        pre=(p['shift_mu'], p['w0'], _block_diag2(p['w_up'][0], p['w_up'][1]).astype(BF16),
             p['a0'], _block_diag2(p['a_up'][0], p['a_up'][1]).astype(BF16), p['g_up'].astype(BF16),
             row(p['k_k']), row(p['k_a']), row(p['r_k']), ones_bd),
        scan=(jnp.concatenate([ones_pair, ones_pair], axis=0), _block_diag2(ones_pair, ones_pair)),
        out=(row(p['attn_out_norm']), row(p['ln_x_w']), row(p['ln_x_b']), ones_bd,
             p['w_out'][:ATT_WIDTH].astype(BF16), p['w_out'][ATT_WIDTH:].astype(BF16)),
    )


def _before_scan(x, w, final_g):
    b, s, _ = x.shape
    h = _ffn(x.reshape(b * s, D_MODEL), *w['ffn1'], final_g, final=False)
    cos_t, sin_t = _rope_tables(s)
    q, k, v, zr = _inproj(h, s, *w['inproj'], cos_t, sin_t)
    att = _attention(q.reshape(b, s, QK_WIDTH), k.reshape(b, s, QK_WIDTH), v.reshape(b, s, QK_WIDTH))
    *scan_in, g, bonus = _rwkv_pre(zr.reshape(b, s, RWKV_COLS), *w['pre'])
    return h, att, g, bonus, tuple(scan_in)


def _after_scan(shape, h, att, g, bonus, y_f, y_b, w, final_g):
    flat = lambda a: a.reshape(-1, RWKV_WIDTH)
    h = _mix_out(flat(att), flat(y_f), flat(y_b), flat(bonus), flat(g), h, *w['out'])
    return _ffn(h, *w['ffn2'], final_g, final=True).reshape(shape)


def _scan_all(scan_p, scan_s, w):
    bp, sp, _ = scan_p[0].shape
    bs, ss, _ = scan_s[0].shape
    segments = ss // sp
    if bs == 1 and ss == segments * sp and bp % segments == 0:
        offsets = list(range(0, bp, segments))
        outs = _rwkv_scan([(scan_p, off) for off in offsets] + [(scan_s, None)], segments, *w['scan'])
        y_p = [jnp.concatenate(outs[d:2 * len(offsets):2], axis=0) for d in range(2)]
        return y_p, outs[2 * len(offsets):]
    outs_p = _rwkv_scan([(scan_p, off) for off in range(bp)], 1, *w['scan'])
    y_p = [jnp.concatenate(outs_p[d::2], axis=0) for d in range(2)]
    outs_s = _rwkv_scan([(tuple(a[i:i + 1] for a in scan_s), None) for i in range(bs)], 1, *w['scan'])
    return y_p, [jnp.concatenate(outs_s[d::2], axis=0) for d in range(2)]


def kernel(x_prompt, x_sample, ffn1_norm, ffn1_w_gate, ffn1_w_up, ffn1_w_down, mix_norm, w_in, q_norm, w_uq,
           kv_norm, w_ukv, attn_out_norm, shift_mu, w0, w_up, a0, a_up, g_up, k_k, k_a, r_k, ln_x_w, ln_x_b,
           w_out, ffn2_norm, ffn2_w_gate, ffn2_w_up, ffn2_w_down, final_norm):
    assert ffn1_norm.shape[0] == 1, "single-layer trunk"
    p = dict(ffn1_norm=ffn1_norm[0], ffn1_w_gate=ffn1_w_gate[0], ffn1_w_up=ffn1_w_up[0],
             ffn1_w_down=ffn1_w_down[0], mix_norm=mix_norm[0], w_in=w_in[0], q_norm=q_norm[0], w_uq=w_uq[0],
             kv_norm=kv_norm[0], w_ukv=w_ukv[0], attn_out_norm=attn_out_norm[0], shift_mu=shift_mu[0],
             w0=w0[0], w_up=w_up[0], a0=a0[0], a_up=a_up[0], g_up=g_up[0], k_k=k_k[0], k_a=k_a[0],
             r_k=r_k[0], ln_x_w=ln_x_w[0], ln_x_b=ln_x_b[0], w_out=w_out[0], ffn2_norm=ffn2_norm[0],
             ffn2_w_gate=ffn2_w_gate[0], ffn2_w_up=ffn2_w_up[0], ffn2_w_down=ffn2_w_down[0])
    w = _prepare(p)
    final_g = final_norm.reshape(1, D_MODEL)
    *pre_p, scan_p = _before_scan(x_prompt, w, final_g)
    *pre_s, scan_s = _before_scan(x_sample, w, final_g)
    y_p, y_s = _scan_all(scan_p, scan_s, w)
    return (_after_scan(x_prompt.shape, *pre_p, *y_p, w, final_g),
            _after_scan(x_sample.shape, *pre_s, *y_s, w, final_g))
```

```python
import functools
import math

import jax
import jax.numpy as jnp
from jax import lax
from jax.experimental import pallas as pl
from jax.experimental.pallas import tpu as pltpu

F32 = jnp.float32
BF16 = jnp.bfloat16

D_MODEL = 1024
ATT_HEADS = 8
NOPE_DIM = 64
ROPE_DIM = 32
QK_DIM = NOPE_DIM + ROPE_DIM
V_DIM = 64
Q_RANK = 384
KV_RANK = 256
ATT_WIDTH = ATT_HEADS * V_DIM
ROPE_THETA = 10000.0
RWKV_HEADS = 8
RWKV_HEAD_DIM = 64
RWKV_WIDTH = RWKV_HEADS * RWKV_HEAD_DIM
DECAY_LORA = 64
ICLR_LORA = 64
GATE_LORA = 128
D_FF = 2816
NORM_EPS = 1e-6
LN_X_EPS = 64e-5
RWKV_COLS = 3 * RWKV_WIDTH + 2 * DECAY_LORA + 2 * ICLR_LORA + GATE_LORA

LANES = 128
SUBLANES = 8
HEAD_PAD = LANES
QK_WIDTH = ATT_HEADS * HEAD_PAD
FF_CHUNK = 256
N_FF_CHUNKS = D_FF // FF_CHUNK
VMEM_LIMIT = 56 * 1024 * 1024

ROW_TILE = 512
PRE_TILE = 256
ATT_TQ = 512
ATT_TK = 1024
ONES_LANE = (V_DIM, 0)
SCAN_TB = 128


def _rms(x, g):
    return x * lax.rsqrt(jnp.mean(x * x, axis=-1, keepdims=True) + NORM_EPS) * g


def _dot(a, b):
    return jnp.dot(a, b, preferred_element_type=F32)


def _split_bf16(x):
    hi = x.astype(BF16)
    mid = (x - hi.astype(F32)).astype(BF16)
    return hi, mid


def _segsum(x, ones_bd):
    hi, mid = _split_bf16(x)
    return _dot(hi, ones_bd) + _dot(mid, ones_bd)


def _const_spec(shape):
    return pl.BlockSpec(shape, lambda *_: (0,) * len(shape), pipeline_mode=pl.Buffered(1))


def _params(*sem):
    return pltpu.CompilerParams(dimension_semantics=sem, vmem_limit_bytes=VMEM_LIMIT)


def _ffn_body(x_ref, g_ref, wg_ref, wu_ref, wd_ref, fg_ref, o_ref, acc_ref, *, final):
    x = x_ref[...]
    u = _rms(x, g_ref[...]).astype(BF16)
    acc_ref[...] = jnp.zeros_like(acc_ref)

    def chunk(j, carry):
        gate = _dot(u, wg_ref[j])
        up = _dot(u, wu_ref[j])
        act = (gate * jax.nn.sigmoid(gate) * up).astype(BF16)
        acc_ref[...] += _dot(act, wd_ref[j])
        return carry

    lax.fori_loop(0, N_FF_CHUNKS, chunk, 0)
    h = x + 0.5 * acc_ref[...]
    if final:
        h = _rms(h, fg_ref[...])
    o_ref[...] = h


def _ffn(x, norm_g, wg, wu, wd, final_g, *, final):
    t = x.shape[0]
    tm = min(ROW_TILE, t)
    row = pl.BlockSpec((tm, D_MODEL), lambda i: (i, 0))
    return pl.pallas_call(
        functools.partial(_ffn_body, final=final),
        grid=(t // tm,),
        in_specs=[row, _const_spec((1, D_MODEL)),
                  _const_spec((N_FF_CHUNKS, D_MODEL, FF_CHUNK)),
                  _const_spec((N_FF_CHUNKS, D_MODEL, FF_CHUNK)),
                  _const_spec((N_FF_CHUNKS, FF_CHUNK, D_MODEL)),
                  _const_spec((1, D_MODEL))],
        out_specs=row,
        out_shape=jax.ShapeDtypeStruct((t, D_MODEL), F32),
        scratch_shapes=[pltpu.VMEM((tm, D_MODEL), F32)],
        compiler_params=_params("arbitrary"),
        name="ffn_final" if final else "ffn",
    )(x, norm_g, wg, wu, wd, final_g)


IN_ARR_COLS = Q_RANK + KV_RANK + 2 * LANES + RWKV_COLS
OFF_KR_ARR = Q_RANK + KV_RANK
OFF_RWKV_ARR = OFF_KR_ARR + 2 * LANES


def _inproj_body(h_ref, g_ref, win_ref, qn_ref, wuq_ref, kvn_ref, wkv_ref, cos_ref, sin_ref,
                 q_ref, k_ref, v_ref, zr_ref):
    u = _rms(h_ref[...], g_ref[...]).astype(BF16)
    z = _dot(u, win_ref[...])
    zr_ref[...] = z[:, OFF_RWKV_ARR:]

    cos = cos_ref[...]
    sin = sin_ref[...]
    cos_h = jnp.concatenate([cos] * ATT_HEADS, axis=1)
    sin_h = jnp.concatenate([sin] * ATT_HEADS, axis=1)

    q2 = _dot(_rms(z[:, :Q_RANK], qn_ref[...]).astype(BF16), wuq_ref[...])
    q = (q2[:, :QK_WIDTH] * cos_h + q2[:, QK_WIDTH:] * sin_h) * (QK_DIM ** -0.5 * math.log2(math.e))
    q_ref[...] = q.astype(BF16)

    kv = _dot(_rms(z[:, Q_RANK:OFF_KR_ARR], kvn_ref[...]).astype(BF16), wkv_ref[...])
    k_pe = z[:, OFF_KR_ARR:OFF_KR_ARR + LANES] * cos + z[:, OFF_KR_ARR + LANES:OFF_RWKV_ARR] * sin
    k = kv[:, :QK_WIDTH] + jnp.concatenate([k_pe] * ATT_HEADS, axis=1)
    k_ref[...] = k.astype(BF16)
    v = kv[:, QK_WIDTH:]
    col = lax.broadcasted_iota(jnp.int32, v.shape, 1) % (2 * HEAD_PAD)
    is_ones = (col == ONES_LANE[0]) | (col == HEAD_PAD + ONES_LANE[1])
    v_ref[...] = jnp.where(is_ones, 1.0, v).astype(BF16)


def _inproj(h, seq, mix_g, w_in_arr, q_norm, w_uq2, kv_norm, w_kv2, cos_t, sin_t):
    t = h.shape[0]
    tm = min(ROW_TILE, seq)
    per_seq = seq // tm
    row = lambda w: pl.BlockSpec((tm, w), lambda i: (i, 0))
    pos = pl.BlockSpec((tm, LANES), lambda i: (i % per_seq, 0))
    return pl.pallas_call(
        _inproj_body,
        grid=(t // tm,),
        in_specs=[row(D_MODEL), _const_spec((1, D_MODEL)), _const_spec((D_MODEL, IN_ARR_COLS)),
                  _const_spec((1, Q_RANK)), _const_spec((Q_RANK, 2 * QK_WIDTH)),
                  _const_spec((1, KV_RANK)), _const_spec((KV_RANK, 2 * QK_WIDTH)),
                  pos, pos],
        out_specs=[row(QK_WIDTH), row(QK_WIDTH), row(QK_WIDTH), row(RWKV_COLS)],
        out_shape=[jax.ShapeDtypeStruct((t, QK_WIDTH), BF16),
                   jax.ShapeDtypeStruct((t, QK_WIDTH), BF16),
                   jax.ShapeDtypeStruct((t, QK_WIDTH), BF16),
                   jax.ShapeDtypeStruct((t, RWKV_COLS), F32)],
        compiler_params=_params("arbitrary"),
        name="inproj",
    )(h, mix_g, w_in_arr, q_norm, w_uq2, kv_norm, w_kv2, cos_t, sin_t)


def _attn_body(q_ref, k_ref, v_ref, o_ref, *, tk, nk):
    tq = q_ref.shape[1]
    head_lanes = [slice(hh * HEAD_PAD, (hh + 1) * HEAD_PAD) for hh in range(2)]
    qs = [q_ref[0, :, lanes] for lanes in head_lanes]

    def step(j, carry):
        rows = pl.ds(pl.multiple_of(j * tk, tk), tk)
        new = []
        scores = [lax.dot_general(qs[hh], k_ref[0, rows, head_lanes[hh]], (((1,), (1,)), ((), ())),
                                  preferred_element_type=F32) for hh in range(2)]
        for hh in range(2):
            m, acc = carry[hh]
            vs = v_ref[0, rows, head_lanes[hh]]
            s = scores[hh]
            m_new = jnp.maximum(m, jnp.max(s, axis=-1, keepdims=True))
            p = jnp.exp2(s - m_new)
            acc = jnp.exp2(m - m_new) * acc + _dot(p.astype(BF16), vs)
            new.append((m_new, acc))
        return tuple(new)

    init = (jnp.full((tq, 1), -jnp.inf, F32), jnp.zeros((tq, LANES), F32))
    res = lax.fori_loop(0, nk, step, (init, init))
    lane = lax.broadcasted_iota(jnp.int32, (tq, LANES), 1)
    out = jnp.zeros((tq, LANES), F32)
    for hh in range(2):
        acc = res[hh][1]
        ones_lane = ONES_LANE[hh]
        is_v = (lane < V_DIM) if hh == 0 else (lane >= V_DIM)
        out = out + jnp.where(is_v, acc / acc[:, ones_lane:ones_lane + 1], 0.0)
    o_ref[0] = out


def _attention(q, k, v):
    b, s, _ = q.shape
    tq = min(ATT_TQ, s)
    tk = min(ATT_TK, s)
    pairs = ATT_HEADS // 2
    return pl.pallas_call(
        functools.partial(_attn_body, tk=tk, nk=s // tk),
        grid=(b, pairs, s // tq),
        in_specs=[pl.BlockSpec((1, tq, 2 * HEAD_PAD), lambda bi, hp, qi: (bi, qi, hp)),
                  pl.BlockSpec((1, s, 2 * HEAD_PAD), lambda bi, hp, qi: (bi, 0, hp),
                               pipeline_mode=pl.Buffered(1)),
                  pl.BlockSpec((1, s, 2 * HEAD_PAD), lambda bi, hp, qi: (bi, 0, hp),
                               pipeline_mode=pl.Buffered(1))],
        out_specs=pl.BlockSpec((1, tq, LANES), lambda bi, hp, qi: (bi, qi, hp)),
        out_shape=jax.ShapeDtypeStruct((b, s, ATT_WIDTH), F32),
        compiler_params=_params("arbitrary", "arbitrary", "arbitrary"),
        name="attention",
    )(q, k, v)


EXP_NEG_HALF = math.exp(-0.5)


def _rwkv_pre_body(z_ref, zp_ref, zn_ref, mu_ref, w0_ref, wup_ref, a0_ref, aup_ref, gup_ref,
                   kk_ref, ka_ref, rk_ref, bd_ref,
                   r_out, v_out, kk_out, wf_out, wb_out, kdf_out, kdb_out, bf_out, bb_out, g_out, bonus_out):
    i = pl.program_id(1)
    n = pl.num_programs(1)
    z = z_ref[0]
    ts = z.shape[0]
    prev_row = jnp.where(i > 0, zp_ref[0, SUBLANES - 1:SUBLANES, :], 0.0)
    next_row = jnp.where(i < n - 1, zn_ref[0, 0:1, :], 0.0)
    rows = lax.broadcasted_iota(jnp.int32, z.shape, 0)
    z_prev = jnp.where(rows == 0, prev_row, pltpu.roll(z, 1, 0))
    z_next = jnp.where(rows == ts - 1, next_row, pltpu.roll(z, ts - 1, 0))
    zs = z + mu_ref[0:1, :] * (z_prev - z) + mu_ref[1:2, :] * (z_next - z)

    w = RWKV_WIDTH
    r = zs[:, 0:w]
    k = zs[:, w:2 * w]
    v = zs[:, 2 * w:3 * w]
    wd = zs[:, 3 * w:3 * w + LANES]
    ad = zs[:, 3 * w + LANES:3 * w + 2 * LANES]
    gd = zs[:, 3 * w + 2 * LANES:]

    lora_w = _dot(jnp.tanh(wd).astype(BF16), wup_ref[...])
    lora_a = _dot(ad.astype(BF16), aup_ref[...])
    g_out[0] = _dot(jax.nn.sigmoid(gd).astype(BF16), gup_ref[...])

    ones_bd = bd_ref[...]
    kkf = k * kk_ref[...]
    kk = kkf / jnp.maximum(jnp.sqrt(_segsum(kkf * kkf, ones_bd)), 1e-12)
    r_out[0] = r
    v_out[0] = v
    kk_out[0] = kk

    k_sum = jnp.zeros_like(k)
    for d, (w_out, kd_out, b_out) in enumerate(((wf_out, kdf_out, bf_out), (wb_out, kdb_out, bb_out))):
        x = w0_ref[d:d + 1, :] + lora_w[:, d * w:(d + 1) * w]
        w_out[0] = jnp.exp(-EXP_NEG_HALF * jax.nn.sigmoid(x))
        a = jax.nn.sigmoid(a0_ref[d:d + 1, :] + lora_a[:, d * w:(d + 1) * w])
        k_d = k * (1.0 + (a - 1.0) * ka_ref[...])
        kd_out[0] = k_d
        b_out[0] = kk * a
        k_sum = k_sum + k_d
    bonus_out[0] = _segsum(r * k_sum * rk_ref[...], ones_bd) * v


def _rwkv_pre(zr, shift_mu, w0, wup_bd, a0, aup_bd, g_up, k_k, k_a, r_k, ones_bd):
    b, s, _ = zr.shape
    ts = min(PRE_TILE, s)
    halo = ts // SUBLANES
    last = s // SUBLANES - 1
    wide = lambda: pl.BlockSpec((1, ts, RWKV_WIDTH), lambda bi, i: (bi, i, 0))
    out = jax.ShapeDtypeStruct((b, s, RWKV_WIDTH), F32)
    return pl.pallas_call(
        _rwkv_pre_body,
        grid=(b, s // ts),
        in_specs=[pl.BlockSpec((1, ts, RWKV_COLS), lambda bi, i: (bi, i, 0)),
                  pl.BlockSpec((1, SUBLANES, RWKV_COLS), lambda bi, i: (bi, jnp.maximum(i * halo - 1, 0), 0)),
                  pl.BlockSpec((1, SUBLANES, RWKV_COLS), lambda bi, i: (bi, jnp.minimum((i + 1) * halo, last), 0)),
                  _const_spec((2, RWKV_COLS)),
                  _const_spec((2, RWKV_WIDTH)), _const_spec((LANES, 2 * RWKV_WIDTH)),
                  _const_spec((2, RWKV_WIDTH)), _const_spec((LANES, 2 * RWKV_WIDTH)),
                  _const_spec((GATE_LORA, RWKV_WIDTH)),
                  _const_spec((1, RWKV_WIDTH)), _const_spec((1, RWKV_WIDTH)), _const_spec((1, RWKV_WIDTH)),
                  _const_spec((RWKV_WIDTH, RWKV_WIDTH))],
        out_specs=[wide() for _ in range(11)],
        out_shape=[out] * 11,
        compiler_params=_params("arbitrary", "arbitrary"),
        name="rwkv_pre",
    )(zr, zr, zr, shift_mu, w0, wup_bd, a0, aup_bd, g_up, k_k, k_a, r_k, ones_bd)


HEAD_PAIRS = RWKV_HEADS // 2
N_CHAINS = 2 * HEAD_PAIRS
STATE_VREGS = RWKV_HEAD_DIM // SUBLANES


def _scan_body(*refs, tb, long_stream):
    n_streams = len(long_stream)
    ins = refs[:12 * n_streams]
    w2_ref, wo_ref = refs[12 * n_streams:12 * n_streams + 2]
    outs = refs[12 * n_streams + 2:14 * n_streams + 2]
    state_ref = refs[-1]

    for si, is_long in enumerate(long_stream):
        first = pl.program_id(1) == 0
        if is_long:
            first = first & (pl.program_id(0) == 0)

        @pl.when(first)
        def _(si=si):
            state_ref[si * N_CHAINS:(si + 1) * N_CHAINS] = jnp.zeros((N_CHAINS,) + state_ref.shape[1:], F32)

    w2 = w2_ref[...]
    wo = wo_ref[...]
    shape3 = (STATE_VREGS, SUBLANES, LANES)
    vrow = lax.broadcasted_iota(jnp.int32, shape3, 0) * SUBLANES + lax.broadcasted_iota(jnp.int32, shape3, 1)
    diag = vrow == lax.broadcasted_iota(jnp.int32, shape3, 2) % RWKV_HEAD_DIM

    n = RWKV_HEAD_DIM

    def group_sums(xs):
        parts = []
        for x3 in xs:
            hi, mid = _split_bf16(x3.reshape(n, LANES))
            parts.append(jnp.concatenate([hi, mid], axis=1))
        res = _dot(jnp.concatenate(parts, axis=0), w2)
        return [res[i * n:(i + 1) * n].reshape(shape3) for i in range(len(xs))]

    def paired_sums(xs):
        xb = [x3.reshape(n, LANES).astype(BF16) for x3 in xs]
        lhs = jnp.concatenate([jnp.concatenate(xb[i:i + 2], axis=1) for i in range(0, len(xb), 2)], axis=0)
        res = _dot(lhs, wo)
        return [res[(i // 2) * n:(i // 2 + 1) * n, (i % 2) * LANES:(i % 2 + 1) * LANES].reshape(shape3)
                for i in range(len(xs))]

    in_refs = [[ins[12 * si + 6 * d:12 * si + 6 * d + 6] for d in range(2)] for si in range(n_streams)]
    out_refs = [outs[2 * si:2 * si + 2] for si in range(n_streams)]

    n_groups = tb // SUBLANES
    sub = lax.broadcasted_iota(jnp.int32, (SUBLANES, LANES), 0)
    lane_k = lax.broadcasted_iota(jnp.int32, (SUBLANES, LANES), 1) % RWKV_HEAD_DIM
    lane_vreg = lane_k // SUBLANES
    lane_sub = lane_k % SUBLANES

    chains = [(si, d, hp) for si in range(n_streams) for d in range(2) for hp in range(HEAD_PAIRS)]
    n_chains = len(chains)
    groups = [range(si * N_CHAINS, (si + 1) * N_CHAINS) for si in range(n_streams)]

    def token_group(gi, carry):
        rows8 = [pl.ds(pl.multiple_of(g * SUBLANES, SUBLANES), SUBLANES) for g in (gi, n_groups - 1 - gi)]
        tiles = [[ref[0, rows8[d], hp * LANES:(hp + 1) * LANES] for ref in in_refs[si][d]] for si, d, hp in chains]
        states = [state_ref[c] for c in range(n_chains)]
        y_tiles = [jnp.zeros((SUBLANES, LANES), F32)] * n_chains
        for jj in range(SUBLANES):
            rows = []
            for c, (si, d, hp) in enumerate(chains):
                j = jj if d == 0 else SUBLANES - 1 - jj
                rows.append([t[j:j + 1, :][None] for t in tiles[c]])
            sa, v_col = [], []
            for cs in groups:
                sa += paired_sums([states[c] * rows[c][0] for c in cs])
            for cs in groups:
                v_col += paired_sums([jnp.where(diag, rows[c][5], 0.0) for c in cs])
            for c in range(n_chains):
                _, w, b, kd, _, _ = rows[c]
                states[c] = states[c] * w - sa[c] * b + v_col[c] * kd
            for cs in groups:
                o = paired_sums([states[c] * rows[c][4] for c in cs])
                for i, c in enumerate(cs):
                    d = chains[c][1]
                    j = jj if d == 0 else SUBLANES - 1 - jj
                    pick = o[i][0]
                    for vi in range(1, STATE_VREGS):
                        pick = jnp.where(lane_vreg == vi, o[i][vi], pick)
                    y = jnp.sum(jnp.where(lane_sub == sub, pick, 0.0), axis=0, keepdims=True)
                    y_tiles[c] = jnp.where(sub == j, y, y_tiles[c])
        for c, (si, d, hp) in enumerate(chains):
            state_ref[c] = states[c]
            out_refs[si][d][0, rows8[d], hp * LANES:(hp + 1) * LANES] = y_tiles[c]
        return carry

    lax.fori_loop(0, n_groups, token_group, 0)


def _rwkv_scan(streams, segments, w2, wo):
    seg_len = min(a[0].shape[1] for a, _ in streams)
    tb = min(SCAN_TB, seg_len)
    nb = seg_len // tb
    total = segments * nb
    in_specs, operands, out_specs, out_shapes, long_stream = [], [], [], [], []
    for arrays, off in streams:
        r, v, kk, w_f, w_b, kd_f, kd_b, b_f, b_b = arrays
        if off is None:
            assert r.shape[0] == 1 and r.shape[1] == segments * seg_len
            fwd_map = lambda h, j: (0, h * nb + j, 0)
            bwd_map = lambda h, j: (0, total - 1 - (h * nb + j), 0)
            out_shape = r.shape
        else:
            assert r.shape[1] == seg_len and off + segments <= r.shape[0]
            fwd_map = lambda h, j, off=off: (off + h, j, 0)
            bwd_map = lambda h, j, off=off: (off + h, nb - 1 - j, 0)
            out_shape = (segments, seg_len, RWKV_WIDTH)
        fwd = pl.BlockSpec((1, tb, RWKV_WIDTH), fwd_map)
        bwd = pl.BlockSpec((1, tb, RWKV_WIDTH), bwd_map)
        in_specs += [fwd] * 6 + [bwd] * 6
        operands += [kk, w_f, b_f, kd_f, r, v, kk, w_b, b_b, kd_b, r, v]
        out_fwd = fwd if off is None else pl.BlockSpec((1, tb, RWKV_WIDTH), lambda h, j: (h, j, 0))
        out_bwd = bwd if off is None else pl.BlockSpec((1, tb, RWKV_WIDTH), lambda h, j: (h, nb - 1 - j, 0))
        out_specs += [out_fwd, out_bwd]
        out_shapes += [jax.ShapeDtypeStruct(out_shape, F32)] * 2
        long_stream.append(off is None)
    n_streams = len(streams)
    return pl.pallas_call(
        functools.partial(_scan_body, tb=tb, long_stream=tuple(long_stream)),
        grid=(segments, nb),
        in_specs=in_specs + [_const_spec((2 * LANES, LANES)), _const_spec((2 * LANES, 2 * LANES))],
        out_specs=out_specs,
        out_shape=out_shapes,
        scratch_shapes=[pltpu.VMEM((n_streams * N_CHAINS, STATE_VREGS, SUBLANES, LANES), F32)],
        compiler_params=_params("arbitrary", "arbitrary"),
        name="rwkv_scan",
    )(*operands, w2, wo)


def _mix_out_body(att_ref, yf_ref, yb_ref, bonus_ref, g_ref, h_ref, an_ref, lnw_ref, lnb_ref, bd_ref,
                  wo_att_ref, wo_tm_ref, o_ref):
    att = _rms(att_ref[...], an_ref[...])
    ones_bd = bd_ref[...]
    y = yf_ref[...] + yb_ref[...]
    inv_n = 1.0 / RWKV_HEAD_DIM
    c = y - _segsum(y, ones_bd) * inv_n
    var = _segsum(c * c, ones_bd) * inv_n
    yn = c * lax.rsqrt(var + LN_X_EPS) * lnw_ref[...] + lnb_ref[...]
    tm = (yn + bonus_ref[...]) * g_ref[...]
    o_ref[...] = (h_ref[...] + _dot(att.astype(BF16), wo_att_ref[...])
                  + _dot(tm.astype(BF16), wo_tm_ref[...]))


def _mix_out(att, yf, yb, bonus, g, h, attn_norm, ln_w, ln_b, ones_bd, wo_att, wo_tm):
    t = h.shape[0]
    tm = min(ROW_TILE, t)
    half = pl.BlockSpec((tm, RWKV_WIDTH), lambda i: (i, 0))
    full = pl.BlockSpec((tm, D_MODEL), lambda i: (i, 0))
    vec = _const_spec((1, RWKV_WIDTH))
    return pl.pallas_call(
        _mix_out_body,
        grid=(t // tm,),
        in_specs=[half] * 5 + [full, vec, vec, vec, _const_spec((RWKV_WIDTH, RWKV_WIDTH)),
                               _const_spec((ATT_WIDTH, D_MODEL)), _const_spec((RWKV_WIDTH, D_MODEL))],
        out_specs=full,
        out_shape=jax.ShapeDtypeStruct((t, D_MODEL), F32),
        compiler_params=_params("arbitrary"),
        name="mix_out",
    )(att, yf, yb, bonus, g, h, attn_norm, ln_w, ln_b, ones_bd, wo_att, wo_tm)


def _rope_tables(seq):
    inv = 1.0 / (ROPE_THETA ** (jnp.arange(0, ROPE_DIM, 2, dtype=F32) / ROPE_DIM))
    ang = jnp.arange(seq, dtype=F32)[:, None] * inv[None, :]
    cos, sin = jnp.cos(ang), jnp.sin(ang)
    pad = jnp.zeros((seq, HEAD_PAD - QK_DIM), F32)
    cos_t = jnp.concatenate([jnp.ones((seq, NOPE_DIM), F32), cos, cos, pad], axis=1)
    sin_t = jnp.concatenate([jnp.zeros((seq, NOPE_DIM), F32), sin, sin, pad], axis=1)
    return cos_t, sin_t


def _rotate_half_cols(w):
    half = ROPE_DIM // 2
    return jnp.concatenate([-w[..., half:], w[..., :half]], axis=-1)


def _pad_head_cols(nope, rope):
    rows = nope.shape[0]
    pad = jnp.zeros((rows, ATT_HEADS, HEAD_PAD - QK_DIM), nope.dtype)
    return jnp.concatenate([nope, rope, pad], axis=-1).reshape(rows, QK_WIDTH)


def _block_diag2(a, b):
    za = jnp.zeros_like(a)
    return jnp.concatenate([jnp.concatenate([a, za], axis=1), jnp.concatenate([za, b], axis=1)], axis=0)


def _prepare(p):
    row = lambda a: a.reshape(1, -1)
    ffn_w = lambda wg, wu, wd: (
        wg.reshape(D_MODEL, N_FF_CHUNKS, FF_CHUNK).transpose(1, 0, 2).astype(BF16),
        wu.reshape(D_MODEL, N_FF_CHUNKS, FF_CHUNK).transpose(1, 0, 2).astype(BF16),
        wd.reshape(N_FF_CHUNKS, FF_CHUNK, D_MODEL).astype(BF16))

    w_in = p['w_in']
    off_ckv = Q_RANK
    off_kr = off_ckv + KV_RANK
    off_rwkv = off_kr + ROPE_DIM
    w_kr = w_in[:, off_kr:off_rwkv]
    kr_pad = lambda w: jnp.concatenate(
        [jnp.zeros((D_MODEL, NOPE_DIM), F32), w, jnp.zeros((D_MODEL, HEAD_PAD - QK_DIM), F32)], axis=1)
    w_in_arr = jnp.concatenate(
        [w_in[:, :off_kr], kr_pad(w_kr), kr_pad(_rotate_half_cols(w_kr)), w_in[:, off_rwkv:]], axis=1).astype(BF16)

    w_uq = p['w_uq'].reshape(Q_RANK, ATT_HEADS, QK_DIM)
    q_nope, q_rope = w_uq[..., :NOPE_DIM], w_uq[..., NOPE_DIM:]
    w_uq2 = jnp.concatenate(
        [_pad_head_cols(q_nope, q_rope), _pad_head_cols(jnp.zeros_like(q_nope), _rotate_half_cols(q_rope))],
        axis=1).astype(BF16)

    w_ukv = p['w_ukv'].reshape(KV_RANK, ATT_HEADS, NOPE_DIM + V_DIM)
    k_nope, v_w = w_ukv[..., :NOPE_DIM], w_ukv[..., NOPE_DIM:]
    w_k = _pad_head_cols(k_nope, jnp.zeros((KV_RANK, ATT_HEADS, ROPE_DIM), F32))
    zero_v = jnp.zeros((KV_RANK, ATT_HEADS // 2, V_DIM), F32)
    w_v = jnp.stack([jnp.concatenate([v_w[:, 0::2], zero_v], axis=-1),
                     jnp.concatenate([zero_v, v_w[:, 1::2]], axis=-1)], axis=2).reshape(KV_RANK, QK_WIDTH)
    w_kv2 = jnp.concatenate([w_k, w_v], axis=1).astype(BF16)

    head = jnp.arange(RWKV_WIDTH) // RWKV_HEAD_DIM
    ones_bd = (head[:, None] == head[None, :]).astype(BF16)
    pair = jnp.arange(LANES) // RWKV_HEAD_DIM
    ones_pair = (pair[:, None] == pair[None, :]).astype(BF16)

    return dict(
        ffn1=(row(p['ffn1_norm']),) + ffn_w(p['ffn1_w_gate'], p['ffn1_w_up'], p['ffn1_w_down']),
        ffn2=(row(p['ffn2_norm']),) + ffn_w(p['ffn2_w_gate'], p['ffn2_w_up'], p['ffn2_w_down']),
        inproj=(row(p['mix_norm']), w_in_arr, row(p['q_norm']), w_uq2, row(p['kv_norm']), w_kv2),
        pre=(p['shift_mu'], p['w0'], _block_diag2(p['w_up'][0], p['w_up'][1]).astype(BF16),
             p['a0'], _block_diag2(p['a_up'][0], p['a_up'][1]).astype(BF16), p['g_up'].astype(BF16),
             row(p['k_k']), row(p['k_a']), row(p['r_k']), ones_bd),
        scan=(jnp.concatenate([ones_pair, ones_pair], axis=0), _block_diag2(ones_pair, ones_pair)),
        out=(row(p['attn_out_norm']), row(p['ln_x_w']), row(p['ln_x_b']), ones_bd,
             p['w_out'][:ATT_WIDTH].astype(BF16), p['w_out'][ATT_WIDTH:].astype(BF16)),
    )


def _before_scan(x, w, final_g):
    b, s, _ = x.shape
    h = _ffn(x.reshape(b * s, D_MODEL), *w['ffn1'], final_g, final=False)
    cos_t, sin_t = _rope_tables(s)
    q, k, v, zr = _inproj(h, s, *w['inproj'], cos_t, sin_t)
    att = _attention(q.reshape(b, s, QK_WIDTH), k.reshape(b, s, QK_WIDTH), v.reshape(b, s, QK_WIDTH))
    *scan_in, g, bonus = _rwkv_pre(zr.reshape(b, s, RWKV_COLS), *w['pre'])
    return h, att, g, bonus, tuple(scan_in)


def _after_scan(shape, h, att, g, bonus, y_f, y_b, w, final_g):
    flat = lambda a: a.reshape(-1, RWKV_WIDTH)
    h = _mix_out(flat(att), flat(y_f), flat(y_b), flat(bonus), flat(g), h, *w['out'])
    return _ffn(h, *w['ffn2'], final_g, final=True).reshape(shape)


def _scan_all(scan_p, scan_s, w):
    bp, sp, _ = scan_p[0].shape
    bs, ss, _ = scan_s[0].shape
    segments = ss // sp
    if bs == 1 and ss == segments * sp and bp % segments == 0:
        offsets = list(range(0, bp, segments))
        outs = _rwkv_scan([(scan_p, off) for off in offsets] + [(scan_s, None)], segments, *w['scan'])
        y_p = [jnp.concatenate(outs[d:2 * len(offsets):2], axis=0) for d in range(2)]
        return y_p, outs[2 * len(offsets):]
    outs_p = _rwkv_scan([(scan_p, off) for off in range(bp)], 1, *w['scan'])
    y_p = [jnp.concatenate(outs_p[d::2], axis=0) for d in range(2)]
    outs_s = _rwkv_scan([(tuple(a[i:i + 1] for a in scan_s), None) for i in range(bs)], 1, *w['scan'])
    return y_p, [jnp.concatenate(outs_s[d::2], axis=0) for d in range(2)]


def kernel(x_prompt, x_sample, ffn1_norm, ffn1_w_gate, ffn1_w_up, ffn1_w_down, mix_norm, w_in, q_norm, w_uq,
           kv_norm, w_ukv, attn_out_norm, shift_mu, w0, w_up, a0, a_up, g_up, k_k, k_a, r_k, ln_x_w, ln_x_b,
           w_out, ffn2_norm, ffn2_w_gate, ffn2_w_up, ffn2_w_down, final_norm):
    assert ffn1_norm.shape[0] == 1, "single-layer trunk"
    p = dict(ffn1_norm=ffn1_norm[0], ffn1_w_gate=ffn1_w_gate[0], ffn1_w_up=ffn1_w_up[0],
             ffn1_w_down=ffn1_w_down[0], mix_norm=mix_norm[0], w_in=w_in[0], q_norm=q_norm[0], w_uq=w_uq[0],
             kv_norm=kv_norm[0], w_ukv=w_ukv[0], attn_out_norm=attn_out_norm[0], shift_mu=shift_mu[0],
             w0=w0[0], w_up=w_up[0], a0=a0[0], a_up=a_up[0], g_up=g_up[0], k_k=k_k[0], k_a=k_a[0],
             r_k=r_k[0], ln_x_w=ln_x_w[0], ln_x_b=ln_x_b[0], w_out=w_out[0], ffn2_norm=ffn2_norm[0],
             ffn2_w_gate=ffn2_w_gate[0], ffn2_w_up=ffn2_w_up[0], ffn2_w_down=ffn2_w_down[0])
    w = _prepare(p)
    final_g = final_norm.reshape(1, D_MODEL)
    *pre_p, scan_p = _before_scan(x_prompt, w, final_g)
    *pre_s, scan_s = _before_scan(x_sample, w, final_g)
    y_p, y_s = _scan_all(scan_p, scan_s, w)
    return (_after_scan(x_prompt.shape, *pre_p, *y_p, w, final_g),
            _after_scan(x_sample.shape, *pre_s, *y_s, w, final_g))
```

```python
import functools
import math

import jax
import jax.numpy as jnp
from jax import lax
from jax.experimental import pallas as pl
from jax.experimental.pallas import tpu as pltpu

F32 = jnp.float32
BF16 = jnp.bfloat16

D_MODEL = 1024
ATT_HEADS = 8
NOPE_DIM = 64
ROPE_DIM = 32
QK_DIM = NOPE_DIM + ROPE_DIM
V_DIM = 64
Q_RANK = 384
KV_RANK = 256
ATT_WIDTH = ATT_HEADS * V_DIM
ROPE_THETA = 10000.0
RWKV_HEADS = 8
RWKV_HEAD_DIM = 64
RWKV_WIDTH = RWKV_HEADS * RWKV_HEAD_DIM
DECAY_LORA = 64
ICLR_LORA = 64
GATE_LORA = 128
D_FF = 2816
NORM_EPS = 1e-6
LN_X_EPS = 64e-5
RWKV_COLS = 3 * RWKV_WIDTH + 2 * DECAY_LORA + 2 * ICLR_LORA + GATE_LORA

LANES = 128
SUBLANES = 8
HEAD_PAD = LANES
QK_WIDTH = ATT_HEADS * HEAD_PAD
FF_CHUNK = 256
N_FF_CHUNKS = D_FF // FF_CHUNK
VMEM_LIMIT = 56 * 1024 * 1024

ROW_TILE = 512
PRE_TILE = 256
ATT_TQ = 512
ATT_TK = 1024
ONES_LANE = (V_DIM, 0)
SCAN_TB = 128


def _rms(x, g):
    return x * lax.rsqrt(jnp.mean(x * x, axis=-1, keepdims=True) + NORM_EPS) * g


def _dot(a, b):
    return jnp.dot(a, b, preferred_element_type=F32)


def _split_bf16(x):
    hi = x.astype(BF16)
    mid = (x - hi.astype(F32)).astype(BF16)
    return hi, mid


def _segsum(x, ones_bd):
    hi, mid = _split_bf16(x)
    return _dot(hi, ones_bd) + _dot(mid, ones_bd)


def _const_spec(shape):
    return pl.BlockSpec(shape, lambda *_: (0,) * len(shape), pipeline_mode=pl.Buffered(1))


def _params(*sem):
    return pltpu.CompilerParams(dimension_semantics=sem, vmem_limit_bytes=VMEM_LIMIT)


def _ffn_body(x_ref, g_ref, wg_ref, wu_ref, wd_ref, fg_ref, o_ref, acc_ref, *, final):
    x = x_ref[...]
    u = _rms(x, g_ref[...]).astype(BF16)

    gate_up = lambda j: (_dot(u, wg_ref[j]), _dot(u, wu_ref[j]))
    gate, up = gate_up(0)
    for j in range(N_FF_CHUNKS):
        nxt = gate_up(j + 1) if j + 1 < N_FF_CHUNKS else None
        act = (gate * jax.nn.sigmoid(gate) * up).astype(BF16)
        down = _dot(act, wd_ref[j])
        if j == 0:
            acc_ref[...] = down
        else:
            acc_ref[...] += down
        if nxt is not None:
            gate, up = nxt
    h = x + 0.5 * acc_ref[...]
    if final:
        h = _rms(h, fg_ref[...])
    o_ref[...] = h


def _ffn(x, norm_g, wg, wu, wd, final_g, *, final):
    t = x.shape[0]
    tm = min(ROW_TILE, t)
    row = pl.BlockSpec((tm, D_MODEL), lambda i: (i, 0))
    return pl.pallas_call(
        functools.partial(_ffn_body, final=final),
        grid=(t // tm,),
        in_specs=[row, _const_spec((1, D_MODEL)),
                  _const_spec((N_FF_CHUNKS, D_MODEL, FF_CHUNK)),
                  _const_spec((N_FF_CHUNKS, D_MODEL, FF_CHUNK)),
                  _const_spec((N_FF_CHUNKS, FF_CHUNK, D_MODEL)),
                  _const_spec((1, D_MODEL))],
        out_specs=row,
        out_shape=jax.ShapeDtypeStruct((t, D_MODEL), F32),
        scratch_shapes=[pltpu.VMEM((tm, D_MODEL), F32)],
        compiler_params=_params("arbitrary"),
        name="ffn_final" if final else "ffn",
    )(x, norm_g, wg, wu, wd, final_g)


IN_ARR_COLS = Q_RANK + KV_RANK + 2 * LANES + RWKV_COLS
OFF_KR_ARR = Q_RANK + KV_RANK
OFF_RWKV_ARR = OFF_KR_ARR + 2 * LANES


def _inproj_body(h_ref, g_ref, win_ref, qn_ref, wuq_ref, kvn_ref, wkv_ref, cos_ref, sin_ref,
                 q_ref, k_ref, v_ref, zr_ref):
    u = _rms(h_ref[...], g_ref[...]).astype(BF16)
    z = _dot(u, win_ref[...])
    zr_ref[...] = z[:, OFF_RWKV_ARR:]

    cos = cos_ref[...]
    sin = sin_ref[...]
    cos_h = jnp.concatenate([cos] * ATT_HEADS, axis=1)
    sin_h = jnp.concatenate([sin] * ATT_HEADS, axis=1)

    q2 = _dot(_rms(z[:, :Q_RANK], qn_ref[...]).astype(BF16), wuq_ref[...])
    q = (q2[:, :QK_WIDTH] * cos_h + q2[:, QK_WIDTH:] * sin_h) * (QK_DIM ** -0.5 * math.log2(math.e))
    q_ref[...] = q.astype(BF16)

    kv = _dot(_rms(z[:, Q_RANK:OFF_KR_ARR], kvn_ref[...]).astype(BF16), wkv_ref[...])
    k_pe = z[:, OFF_KR_ARR:OFF_KR_ARR + LANES] * cos + z[:, OFF_KR_ARR + LANES:OFF_RWKV_ARR] * sin
    k = kv[:, :QK_WIDTH] + jnp.concatenate([k_pe] * ATT_HEADS, axis=1)
    k_ref[...] = k.astype(BF16)
    v = kv[:, QK_WIDTH:]
    col = lax.broadcasted_iota(jnp.int32, v.shape, 1) % (2 * HEAD_PAD)
    is_ones = (col == ONES_LANE[0]) | (col == HEAD_PAD + ONES_LANE[1])
    v_ref[...] = jnp.where(is_ones, 1.0, v).astype(BF16)


def _inproj(h, seq, mix_g, w_in_arr, q_norm, w_uq2, kv_norm, w_kv2, cos_t, sin_t):
    t = h.shape[0]
    tm = min(ROW_TILE, seq)
    per_seq = seq // tm
    row = lambda w: pl.BlockSpec((tm, w), lambda i: (i, 0))
    pos = pl.BlockSpec((tm, LANES), lambda i: (i % per_seq, 0))
    return pl.pallas_call(
        _inproj_body,
        grid=(t // tm,),
        in_specs=[row(D_MODEL), _const_spec((1, D_MODEL)), _const_spec((D_MODEL, IN_ARR_COLS)),
                  _const_spec((1, Q_RANK)), _const_spec((Q_RANK, 2 * QK_WIDTH)),
                  _const_spec((1, KV_RANK)), _const_spec((KV_RANK, 2 * QK_WIDTH)),
                  pos, pos],
        out_specs=[row(QK_WIDTH), row(QK_WIDTH), row(QK_WIDTH), row(RWKV_COLS)],
        out_shape=[jax.ShapeDtypeStruct((t, QK_WIDTH), BF16),
                   jax.ShapeDtypeStruct((t, QK_WIDTH), BF16),
                   jax.ShapeDtypeStruct((t, QK_WIDTH), BF16),
                   jax.ShapeDtypeStruct((t, RWKV_COLS), F32)],
        compiler_params=_params("arbitrary"),
        name="inproj",
    )(h, mix_g, w_in_arr, q_norm, w_uq2, kv_norm, w_kv2, cos_t, sin_t)


def _attn_body(q_ref, k_ref, v_ref, o_ref, *, tk, nk):
    tq = q_ref.shape[1]
    head_lanes = [slice(hh * HEAD_PAD, (hh + 1) * HEAD_PAD) for hh in range(2)]
    qs = [q_ref[0, :, lanes] for lanes in head_lanes]

    def step(j, carry):
        rows = pl.ds(pl.multiple_of(j * tk, tk), tk)
        new = []
        scores = [lax.dot_general(qs[hh], k_ref[0, rows, head_lanes[hh]], (((1,), (1,)), ((), ())),
                                  preferred_element_type=F32) for hh in range(2)]
        for hh in range(2):
            m, acc = carry[hh]
            vs = v_ref[0, rows, head_lanes[hh]]
            s = scores[hh]
            m_new = jnp.maximum(m, jnp.max(s, axis=-1, keepdims=True))
            p = jnp.exp2(s - m_new)
            acc = jnp.exp2(m - m_new) * acc + _dot(p.astype(BF16), vs)
            new.append((m_new, acc))
        return tuple(new)

    init = (jnp.full((tq, 1), -jnp.inf, F32), jnp.zeros((tq, LANES), F32))
    res = lax.fori_loop(0, nk, step, (init, init))
    lane = lax.broadcasted_iota(jnp.int32, (tq, LANES), 1)
    out = jnp.zeros((tq, LANES), F32)
    for hh in range(2):
        acc = res[hh][1]
        ones_lane = ONES_LANE[hh]
        is_v = (lane < V_DIM) if hh == 0 else (lane >= V_DIM)
        out = out + jnp.where(is_v, acc / acc[:, ones_lane:ones_lane + 1], 0.0)
    o_ref[0] = out


def _attention(q, k, v):
    b, s, _ = q.shape
    tq = min(ATT_TQ, s)
    tk = min(ATT_TK, s)
    pairs = ATT_HEADS // 2
    return pl.pallas_call(
        functools.partial(_attn_body, tk=tk, nk=s // tk),
        grid=(b, pairs, s // tq),
        in_specs=[pl.BlockSpec((1, tq, 2 * HEAD_PAD), lambda bi, hp, qi: (bi, qi, hp)),
                  pl.BlockSpec((1, s, 2 * HEAD_PAD), lambda bi, hp, qi: (bi, 0, hp),
                               pipeline_mode=pl.Buffered(1)),
                  pl.BlockSpec((1, s, 2 * HEAD_PAD), lambda bi, hp, qi: (bi, 0, hp),
                               pipeline_mode=pl.Buffered(1))],
        out_specs=pl.BlockSpec((1, tq, LANES), lambda bi, hp, qi: (bi, qi, hp)),
        out_shape=jax.ShapeDtypeStruct((b, s, ATT_WIDTH), F32),
        compiler_params=_params("arbitrary", "arbitrary", "arbitrary"),
        name="attention",
    )(q, k, v)


EXP_NEG_HALF = math.exp(-0.5)


def _rwkv_pre_body(z_ref, zp_ref, zn_ref, mu_ref, w0_ref, wup_ref, a0_ref, aup_ref, gup_ref,
                   kk_ref, ka_ref, rk_ref, bd_ref,
                   r_out, v_out, kk_out, wf_out, wb_out, kdf_out, kdb_out, bf_out, bb_out, g_out, bonus_out):
    i = pl.program_id(1)
    n = pl.num_programs(1)
    z = z_ref[0]
    ts = z.shape[0]
    prev_row = jnp.where(i > 0, zp_ref[0, SUBLANES - 1:SUBLANES, :], 0.0)
    next_row = jnp.where(i < n - 1, zn_ref[0, 0:1, :], 0.0)
    rows = lax.broadcasted_iota(jnp.int32, z.shape, 0)
    z_prev = jnp.where(rows == 0, prev_row, pltpu.roll(z, 1, 0))
    z_next = jnp.where(rows == ts - 1, next_row, pltpu.roll(z, ts - 1, 0))
    zs = z + mu_ref[0:1, :] * (z_prev - z) + mu_ref[1:2, :] * (z_next - z)

    w = RWKV_WIDTH
    r = zs[:, 0:w]
    k = zs[:, w:2 * w]
    v = zs[:, 2 * w:3 * w]
    wd = zs[:, 3 * w:3 * w + LANES]
    ad = zs[:, 3 * w + LANES:3 * w + 2 * LANES]
    gd = zs[:, 3 * w + 2 * LANES:]

    lora_w = _dot(jnp.tanh(wd).astype(BF16), wup_ref[...])
    lora_a = _dot(ad.astype(BF16), aup_ref[...])
    g_out[0] = _dot(jax.nn.sigmoid(gd).astype(BF16), gup_ref[...])

    ones_bd = bd_ref[...]
    kkf = k * kk_ref[...]
    kk = kkf / jnp.maximum(jnp.sqrt(_segsum(kkf * kkf, ones_bd)), 1e-12)
    r_out[0] = r
    v_out[0] = v
    kk_out[0] = kk

    k_sum = jnp.zeros_like(k)
    for d, (w_out, kd_out, b_out) in enumerate(((wf_out, kdf_out, bf_out), (wb_out, kdb_out, bb_out))):
        x = w0_ref[d:d + 1, :] + lora_w[:, d * w:(d + 1) * w]
        w_out[0] = jnp.exp(-EXP_NEG_HALF * jax.nn.sigmoid(x))
        a = jax.nn.sigmoid(a0_ref[d:d + 1, :] + lora_a[:, d * w:(d + 1) * w])
        k_d = k * (1.0 + (a - 1.0) * ka_ref[...])
        kd_out[0] = k_d
        b_out[0] = kk * a
        k_sum = k_sum + k_d
    bonus_out[0] = _segsum(r * k_sum * rk_ref[...], ones_bd) * v


def _rwkv_pre(zr, shift_mu, w0, wup_bd, a0, aup_bd, g_up, k_k, k_a, r_k, ones_bd):
    b, s, _ = zr.shape
    ts = min(PRE_TILE, s)
    halo = ts // SUBLANES
    last = s // SUBLANES - 1
    wide = lambda: pl.BlockSpec((1, ts, RWKV_WIDTH), lambda bi, i: (bi, i, 0))
    out = jax.ShapeDtypeStruct((b, s, RWKV_WIDTH), F32)
    return pl.pallas_call(
        _rwkv_pre_body,
        grid=(b, s // ts),
        in_specs=[pl.BlockSpec((1, ts, RWKV_COLS), lambda bi, i: (bi, i, 0)),
                  pl.BlockSpec((1, SUBLANES, RWKV_COLS), lambda bi, i: (bi, jnp.maximum(i * halo - 1, 0), 0)),
                  pl.BlockSpec((1, SUBLANES, RWKV_COLS), lambda bi, i: (bi, jnp.minimum((i + 1) * halo, last), 0)),
                  _const_spec((2, RWKV_COLS)),
                  _const_spec((2, RWKV_WIDTH)), _const_spec((LANES, 2 * RWKV_WIDTH)),
                  _const_spec((2, RWKV_WIDTH)), _const_spec((LANES, 2 * RWKV_WIDTH)),
                  _const_spec((GATE_LORA, RWKV_WIDTH)),
                  _const_spec((1, RWKV_WIDTH)), _const_spec((1, RWKV_WIDTH)), _const_spec((1, RWKV_WIDTH)),
                  _const_spec((RWKV_WIDTH, RWKV_WIDTH))],
        out_specs=[wide() for _ in range(11)],
        out_shape=[out] * 11,
        compiler_params=_params("arbitrary", "arbitrary"),
        name="rwkv_pre",
    )(zr, zr, zr, shift_mu, w0, wup_bd, a0, aup_bd, g_up, k_k, k_a, r_k, ones_bd)


HEAD_PAIRS = RWKV_HEADS // 2
N_CHAINS = 2 * HEAD_PAIRS
STATE_VREGS = RWKV_HEAD_DIM // SUBLANES


def _scan_body(*refs, tb, long_stream):
    n_streams = len(long_stream)
    ins = refs[:12 * n_streams]
    w2_ref, wo_ref = refs[12 * n_streams:12 * n_streams + 2]
    outs = refs[12 * n_streams + 2:14 * n_streams + 2]
    state_ref = refs[-1]

    for si, is_long in enumerate(long_stream):
        first = pl.program_id(1) == 0
        if is_long:
            first = first & (pl.program_id(0) == 0)

        @pl.when(first)
        def _(si=si):
            state_ref[si * N_CHAINS:(si + 1) * N_CHAINS] = jnp.zeros((N_CHAINS,) + state_ref.shape[1:], F32)

    w2 = w2_ref[...]
    wo = wo_ref[...]
    shape3 = (STATE_VREGS, SUBLANES, LANES)
    vrow = lax.broadcasted_iota(jnp.int32, shape3, 0) * SUBLANES + lax.broadcasted_iota(jnp.int32, shape3, 1)
    diag = vrow == lax.broadcasted_iota(jnp.int32, shape3, 2) % RWKV_HEAD_DIM

    n = RWKV_HEAD_DIM

    def group_sums(xs):
        parts = []
        for x3 in xs:
            hi, mid = _split_bf16(x3.reshape(n, LANES))
            parts.append(jnp.concatenate([hi, mid], axis=1))
        res = _dot(jnp.concatenate(parts, axis=0), w2)
        return [res[i * n:(i + 1) * n].reshape(shape3) for i in range(len(xs))]

    def paired_sums(xs):
        xb = [x3.reshape(n, LANES).astype(BF16) for x3 in xs]
        lhs = jnp.concatenate([jnp.concatenate(xb[i:i + 2], axis=1) for i in range(0, len(xb), 2)], axis=0)
        res = _dot(lhs, wo)
        return [res[(i // 2) * n:(i // 2 + 1) * n, (i % 2) * LANES:(i % 2 + 1) * LANES].reshape(shape3)
                for i in range(len(xs))]

    in_refs = [[ins[12 * si + 6 * d:12 * si + 6 * d + 6] for d in range(2)] for si in range(n_streams)]
    out_refs = [outs[2 * si:2 * si + 2] for si in range(n_streams)]

    n_groups = tb // SUBLANES
    sub = lax.broadcasted_iota(jnp.int32, (SUBLANES, LANES), 0)
    lane_k = lax.broadcasted_iota(jnp.int32, (SUBLANES, LANES), 1) % RWKV_HEAD_DIM
    lane_vreg = lane_k // SUBLANES
    lane_sub = lane_k % SUBLANES

    chains = [(si, d, hp) for si in range(n_streams) for d in range(2) for hp in range(HEAD_PAIRS)]
    n_chains = len(chains)
    groups = [range(si * N_CHAINS, (si + 1) * N_CHAINS) for si in range(n_streams)]

    def token_group(gi, carry):
        rows8 = [pl.ds(pl.multiple_of(g * SUBLANES, SUBLANES), SUBLANES) for g in (gi, n_groups - 1 - gi)]
        tiles = [[ref[0, rows8[d], hp * LANES:(hp + 1) * LANES] for ref in in_refs[si][d]] for si, d, hp in chains]
        states = [state_ref[c] for c in range(n_chains)]
        y_tiles = [jnp.zeros((SUBLANES, LANES), F32)] * n_chains
        for jj in range(SUBLANES):
            rows = []
            for c, (si, d, hp) in enumerate(chains):
                j = jj if d == 0 else SUBLANES - 1 - jj
                rows.append([t[j:j + 1, :][None] for t in tiles[c]])
            sa, v_col = [], []
            for cs in groups:
                sa += paired_sums([states[c] * rows[c][0] for c in cs])
            for cs in groups:
                v_col += paired_sums([jnp.where(diag, rows[c][5], 0.0) for c in cs])
            for c in range(n_chains):
                _, w, b, kd, _, _ = rows[c]
                states[c] = states[c] * w - sa[c] * b + v_col[c] * kd
            for cs in groups:
                o = paired_sums([states[c] * rows[c][4] for c in cs])
                for i, c in enumerate(cs):
                    d = chains[c][1]
                    j = jj if d == 0 else SUBLANES - 1 - jj
                    pick = o[i][0]
                    for vi in range(1, STATE_VREGS):
                        pick = jnp.where(lane_vreg == vi, o[i][vi], pick)
                    y = jnp.sum(jnp.where(lane_sub == sub, pick, 0.0), axis=0, keepdims=True)
                    y_tiles[c] = jnp.where(sub == j, y, y_tiles[c])
        for c, (si, d, hp) in enumerate(chains):
            state_ref[c] = states[c]
            out_refs[si][d][0, rows8[d], hp * LANES:(hp + 1) * LANES] = y_tiles[c]
        return carry

    lax.fori_loop(0, n_groups, token_group, 0)


def _rwkv_scan(streams, segments, w2, wo):
    seg_len = min(a[0].shape[1] for a, _ in streams)
    tb = min(SCAN_TB, seg_len)
    nb = seg_len // tb
    total = segments * nb
    in_specs, operands, out_specs, out_shapes, long_stream = [], [], [], [], []
    for arrays, off in streams:
        r, v, kk, w_f, w_b, kd_f, kd_b, b_f, b_b = arrays
        if off is None:
            assert r.shape[0] == 1 and r.shape[1] == segments * seg_len
            fwd_map = lambda h, j: (0, h * nb + j, 0)
            bwd_map = lambda h, j: (0, total - 1 - (h * nb + j), 0)
            out_shape = r.shape
        else:
            assert r.shape[1] == seg_len and off + segments <= r.shape[0]
            fwd_map = lambda h, j, off=off: (off + h, j, 0)
            bwd_map = lambda h, j, off=off: (off + h, nb - 1 - j, 0)
            out_shape = (segments, seg_len, RWKV_WIDTH)
        fwd = pl.BlockSpec((1, tb, RWKV_WIDTH), fwd_map)
        bwd = pl.BlockSpec((1, tb, RWKV_WIDTH), bwd_map)
        in_specs += [fwd] * 6 + [bwd] * 6
        operands += [kk, w_f, b_f, kd_f, r, v, kk, w_b, b_b, kd_b, r, v]
        out_fwd = fwd if off is None else pl.BlockSpec((1, tb, RWKV_WIDTH), lambda h, j: (h, j, 0))
        out_bwd = bwd if off is None else pl.BlockSpec((1, tb, RWKV_WIDTH), lambda h, j: (h, nb - 1 - j, 0))
        out_specs += [out_fwd, out_bwd]
        out_shapes += [jax.ShapeDtypeStruct(out_shape, F32)] * 2
        long_stream.append(off is None)
    n_streams = len(streams)
    return pl.pallas_call(
        functools.partial(_scan_body, tb=tb, long_stream=tuple(long_stream)),
        grid=(segments, nb),
        in_specs=in_specs + [_const_spec((2 * LANES, LANES)), _const_spec((2 * LANES, 2 * LANES))],
        out_specs=out_specs,
        out_shape=out_shapes,
        scratch_shapes=[pltpu.VMEM((n_streams * N_CHAINS, STATE_VREGS, SUBLANES, LANES), F32)],
        compiler_params=_params("arbitrary", "arbitrary"),
        name="rwkv_scan",
    )(*operands, w2, wo)


def _mix_out_body(att_ref, yf_ref, yb_ref, bonus_ref, g_ref, h_ref, an_ref, lnw_ref, lnb_ref, bd_ref,
                  wo_att_ref, wo_tm_ref, o_ref):
    att = _rms(att_ref[...], an_ref[...])
    ones_bd = bd_ref[...]
    y = yf_ref[...] + yb_ref[...]
    inv_n = 1.0 / RWKV_HEAD_DIM
    c = y - _segsum(y, ones_bd) * inv_n
    var = _segsum(c * c, ones_bd) * inv_n
    yn = c * lax.rsqrt(var + LN_X_EPS) * lnw_ref[...] + lnb_ref[...]
    tm = (yn + bonus_ref[...]) * g_ref[...]
    o_ref[...] = (h_ref[...] + _dot(att.astype(BF16), wo_att_ref[...])
                  + _dot(tm.astype(BF16), wo_tm_ref[...]))


def _mix_out(att, yf, yb, bonus, g, h, attn_norm, ln_w, ln_b, ones_bd, wo_att, wo_tm):
    t = h.shape[0]
    tm = min(ROW_TILE, t)
    half = pl.BlockSpec((tm, RWKV_WIDTH), lambda i: (i, 0))
    full = pl.BlockSpec((tm, D_MODEL), lambda i: (i, 0))
    vec = _const_spec((1, RWKV_WIDTH))
    return pl.pallas_call(
        _mix_out_body,
        grid=(t // tm,),
        in_specs=[half] * 5 + [full, vec, vec, vec, _const_spec((RWKV_WIDTH, RWKV_WIDTH)),
                               _const_spec((ATT_WIDTH, D_MODEL)), _const_spec((RWKV_WIDTH, D_MODEL))],
        out_specs=full,
        out_shape=jax.ShapeDtypeStruct((t, D_MODEL), F32),
        compiler_params=_params("arbitrary"),
        name="mix_out",
    )(att, yf, yb, bonus, g, h, attn_norm, ln_w, ln_b, ones_bd, wo_att, wo_tm)


def _rope_tables(seq):
    inv = 1.0 / (ROPE_THETA ** (jnp.arange(0, ROPE_DIM, 2, dtype=F32) / ROPE_DIM))
    ang = jnp.arange(seq, dtype=F32)[:, None] * inv[None, :]
    cos, sin = jnp.cos(ang), jnp.sin(ang)
    pad = jnp.zeros((seq, HEAD_PAD - QK_DIM), F32)
    cos_t = jnp.concatenate([jnp.ones((seq, NOPE_DIM), F32), cos, cos, pad], axis=1)
    sin_t = jnp.concatenate([jnp.zeros((seq, NOPE_DIM), F32), sin, sin, pad], axis=1)
    return cos_t, sin_t


def _rotate_half_cols(w):
    half = ROPE_DIM // 2
    return jnp.concatenate([-w[..., half:], w[..., :half]], axis=-1)


def _pad_head_cols(nope, rope):
    rows = nope.shape[0]
    pad = jnp.zeros((rows, ATT_HEADS, HEAD_PAD - QK_DIM), nope.dtype)
    return jnp.concatenate([nope, rope, pad], axis=-1).reshape(rows, QK_WIDTH)


def _block_diag2(a, b):
    za = jnp.zeros_like(a)
    return jnp.concatenate([jnp.concatenate([a, za], axis=1), jnp.concatenate([za, b], axis=1)], axis=0)


def _prepare(p):
    row = lambda a: a.reshape(1, -1)
    ffn_w = lambda wg, wu, wd: (
        wg.reshape(D_MODEL, N_FF_CHUNKS, FF_CHUNK).transpose(1, 0, 2).astype(BF16),
        wu.reshape(D_MODEL, N_FF_CHUNKS, FF_CHUNK).transpose(1, 0, 2).astype(BF16),
        wd.reshape(N_FF_CHUNKS, FF_CHUNK, D_MODEL).astype(BF16))

    w_in = p['w_in']
    off_ckv = Q_RANK
    off_kr = off_ckv + KV_RANK
    off_rwkv = off_kr + ROPE_DIM
    w_kr = w_in[:, off_kr:off_rwkv]
    kr_pad = lambda w: jnp.concatenate(
        [jnp.zeros((D_MODEL, NOPE_DIM), F32), w, jnp.zeros((D_MODEL, HEAD_PAD - QK_DIM), F32)], axis=1)
    w_in_arr = jnp.concatenate(
        [w_in[:, :off_kr], kr_pad(w_kr), kr_pad(_rotate_half_cols(w_kr)), w_in[:, off_rwkv:]], axis=1).astype(BF16)

    w_uq = p['w_uq'].reshape(Q_RANK, ATT_HEADS, QK_DIM)
    q_nope, q_rope = w_uq[..., :NOPE_DIM], w_uq[..., NOPE_DIM:]
    w_uq2 = jnp.concatenate(
        [_pad_head_cols(q_nope, q_rope), _pad_head_cols(jnp.zeros_like(q_nope), _rotate_half_cols(q_rope))],
        axis=1).astype(BF16)

    w_ukv = p['w_ukv'].reshape(KV_RANK, ATT_HEADS, NOPE_DIM + V_DIM)
    k_nope, v_w = w_ukv[..., :NOPE_DIM], w_ukv[..., NOPE_DIM:]
    w_k = _pad_head_cols(k_nope, jnp.zeros((KV_RANK, ATT_HEADS, ROPE_DIM), F32))
    zero_v = jnp.zeros((KV_RANK, ATT_HEADS // 2, V_DIM), F32)
    w_v = jnp.stack([jnp.concatenate([v_w[:, 0::2], zero_v], axis=-1),
                     jnp.concatenate([zero_v, v_w[:, 1::2]], axis=-1)], axis=2).reshape(KV_RANK, QK_WIDTH)
    w_kv2 = jnp.concatenate([w_k, w_v], axis=1).astype(BF16)

    head = jnp.arange(RWKV_WIDTH) // RWKV_HEAD_DIM
    ones_bd = (head[:, None] == head[None, :]).astype(BF16)
    pair = jnp.arange(LANES) // RWKV_HEAD_DIM
    ones_pair = (pair[:, None] == pair[None, :]).astype(BF16)

    return dict(
        ffn1=(row(p['ffn1_norm']),) + ffn_w(p['ffn1_w_gate'], p['ffn1_w_up'], p['ffn1_w_down']),
        ffn2=(row(p['ffn2_norm']),) + ffn_w(p['ffn2_w_gate'], p['ffn2_w_up'], p['ffn2_w_down']),
        inproj=(row(p['mix_norm']), w_in_arr, row(p['q_norm']), w_uq2, row(p['kv_norm']), w_kv2),
        pre=(p['shift_mu'], p['w0'], _block_diag2(p['w_up'][0], p['w_up'][1]).astype(BF16),
             p['a0'], _block_diag2(p['a_up'][0], p['a_up'][1]).astype(BF16), p['g_up'].astype(BF16),
             row(p['k_k']), row(p['k_a']), row(p['r_k']), ones_bd),
        scan=(jnp.concatenate([ones_pair, ones_pair], axis=0), _block_diag2(ones_pair, ones_pair)),
        out=(row(p['attn_out_norm']), row(p['ln_x_w']), row(p['ln_x_b']), ones_bd,
             p['w_out'][:ATT_WIDTH].astype(BF16), p['w_out'][ATT_WIDTH:].astype(BF16)),
    )


def _before_scan(x, w, final_g):
    b, s, _ = x.shape
    h = _ffn(x.reshape(b * s, D_MODEL), *w['ffn1'], final_g, final=False)
    cos_t, sin_t = _rope_tables(s)
    q, k, v, zr = _inproj(h, s, *w['inproj'], cos_t, sin_t)
    att = _attention(q.reshape(b, s, QK_WIDTH), k.reshape(b, s, QK_WIDTH), v.reshape(b, s, QK_WIDTH))
    *scan_in, g, bonus = _rwkv_pre(zr.reshape(b, s, RWKV_COLS), *w['pre'])
    return h, att, g, bonus, tuple(scan_in)


def _after_scan(shape, h, att, g, bonus, y_f, y_b, w, final_g):
    flat = lambda a: a.reshape(-1, RWKV_WIDTH)
    h = _mix_out(flat(att), flat(y_f), flat(y_b), flat(bonus), flat(g), h, *w['out'])
    return _ffn(h, *w['ffn2'], final_g, final=True).reshape(shape)


def _scan_all(scan_p, scan_s, w):
    bp, sp, _ = scan_p[0].shape
    bs, ss, _ = scan_s[0].shape
    segments = ss // sp
    if bs == 1 and ss == segments * sp and bp % segments == 0:
        offsets = list(range(0, bp, segments))
        outs = _rwkv_scan([(scan_p, off) for off in offsets] + [(scan_s, None)], segments, *w['scan'])
        y_p = [jnp.concatenate(outs[d:2 * len(offsets):2], axis=0) for d in range(2)]
        return y_p, outs[2 * len(offsets):]
    outs_p = _rwkv_scan([(scan_p, off) for off in range(bp)], 1, *w['scan'])
    y_p = [jnp.concatenate(outs_p[d::2], axis=0) for d in range(2)]
    outs_s = _rwkv_scan([(tuple(a[i:i + 1] for a in scan_s), None) for i in range(bs)], 1, *w['scan'])
    return y_p, [jnp.concatenate(outs_s[d::2], axis=0) for d in range(2)]


def kernel(x_prompt, x_sample, ffn1_norm, ffn1_w_gate, ffn1_w_up, ffn1_w_down, mix_norm, w_in, q_norm, w_uq,
           kv_norm, w_ukv, attn_out_norm, shift_mu, w0, w_up, a0, a_up, g_up, k_k, k_a, r_k, ln_x_w, ln_x_b,
           w_out, ffn2_norm, ffn2_w_gate, ffn2_w_up, ffn2_w_down, final_norm):
    assert ffn1_norm.shape[0] == 1, "single-layer trunk"
    p = dict(ffn1_norm=ffn1_norm[0], ffn1_w_gate=ffn1_w_gate[0], ffn1_w_up=ffn1_w_up[0],
             ffn1_w_down=ffn1_w_down[0], mix_norm=mix_norm[0], w_in=w_in[0], q_norm=q_norm[0], w_uq=w_uq[0],
             kv_norm=kv_norm[0], w_ukv=w_ukv[0], attn_out_norm=attn_out_norm[0], shift_mu=shift_mu[0],
             w0=w0[0], w_up=w_up[0], a0=a0[0], a_up=a_up[0], g_up=g_up[0], k_k=k_k[0], k_a=k_a[0],
             r_k=r_k[0], ln_x_w=ln_x_w[0], ln_x_b=ln_x_b[0], w_out=w_out[0], ffn2_norm=ffn2_norm[0],
             ffn2_w_gate=ffn2_w_gate[0], ffn2_w_up=ffn2_w_up[0], ffn2_w_down=ffn2_w_down[0])
    w = _prepare(p)
    final_g = final_norm.reshape(1, D_MODEL)
    *pre_p, scan_p = _before_scan(x_prompt, w, final_g)
    *pre_s, scan_s = _before_scan(x_sample, w, final_g)
    y_p, y_s = _scan_all(scan_p, scan_s, w)
    return (_after_scan(x_prompt.shape, *pre_p, *y_p, w, final_g),
            _after_scan(x_sample.shape, *pre_s, *y_s, w, final_g))
```

```python
import functools
import math

import jax
import jax.numpy as jnp
from jax import lax
from jax.experimental import pallas as pl
from jax.experimental.pallas import tpu as pltpu

F32 = jnp.float32
BF16 = jnp.bfloat16

D_MODEL = 1024
ATT_HEADS = 8
NOPE_DIM = 64
ROPE_DIM = 32
QK_DIM = NOPE_DIM + ROPE_DIM
V_DIM = 64
Q_RANK = 384
KV_RANK = 256
ATT_WIDTH = ATT_HEADS * V_DIM
ROPE_THETA = 10000.0
RWKV_HEADS = 8
RWKV_HEAD_DIM = 64
RWKV_WIDTH = RWKV_HEADS * RWKV_HEAD_DIM
DECAY_LORA = 64
ICLR_LORA = 64
GATE_LORA = 128
D_FF = 2816
NORM_EPS = 1e-6
LN_X_EPS = 64e-5
RWKV_COLS = 3 * RWKV_WIDTH + 2 * DECAY_LORA + 2 * ICLR_LORA + GATE_LORA

LANES = 128
SUBLANES = 8
HEAD_PAD = LANES
QK_WIDTH = ATT_HEADS * HEAD_PAD
FF_CHUNK = 256
N_FF_CHUNKS = D_FF // FF_CHUNK
VMEM_LIMIT = 56 * 1024 * 1024

ROW_TILE = 512
PRE_TILE = 256
ATT_TQ = 512
ATT_TK = 2048
ONES_LANE = (V_DIM, 0)
SCAN_TB = 128


def _rms(x, g):
    return x * lax.rsqrt(jnp.mean(x * x, axis=-1, keepdims=True) + NORM_EPS) * g


def _dot(a, b):
    return jnp.dot(a, b, preferred_element_type=F32)


def _split_bf16(x):
    hi = x.astype(BF16)
    mid = (x - hi.astype(F32)).astype(BF16)
    return hi, mid


def _segsum(x, ones_bd):
    hi, mid = _split_bf16(x)
    return _dot(hi, ones_bd) + _dot(mid, ones_bd)


def _const_spec(shape):
    return pl.BlockSpec(shape, lambda *_: (0,) * len(shape), pipeline_mode=pl.Buffered(1))


def _params(*sem):
    return pltpu.CompilerParams(dimension_semantics=sem, vmem_limit_bytes=VMEM_LIMIT)


def _ffn_body(x_ref, g_ref, wg_ref, wu_ref, wd_ref, fg_ref, o_ref, acc_ref, *, final):
    x = x_ref[...]
    u = _rms(x, g_ref[...]).astype(BF16)

    gate_up = lambda j: (_dot(u, wg_ref[j]), _dot(u, wu_ref[j]))
    gate, up = gate_up(0)
    for j in range(N_FF_CHUNKS):
        nxt = gate_up(j + 1) if j + 1 < N_FF_CHUNKS else None
        act = (gate * jax.nn.sigmoid(gate) * up).astype(BF16)
        down = _dot(act, wd_ref[j])
        if j == 0:
            acc_ref[...] = down
        else:
            acc_ref[...] += down
        if nxt is not None:
            gate, up = nxt
    h = x + 0.5 * acc_ref[...]
    if final:
        h = _rms(h, fg_ref[...])
    o_ref[...] = h


def _ffn(x, norm_g, wg, wu, wd, final_g, *, final):
    t = x.shape[0]
    tm = min(ROW_TILE, t)
    row = pl.BlockSpec((tm, D_MODEL), lambda i: (i, 0))
    return pl.pallas_call(
        functools.partial(_ffn_body, final=final),
        grid=(t // tm,),
        in_specs=[row, _const_spec((1, D_MODEL)),
                  _const_spec((N_FF_CHUNKS, D_MODEL, FF_CHUNK)),
                  _const_spec((N_FF_CHUNKS, D_MODEL, FF_CHUNK)),
                  _const_spec((N_FF_CHUNKS, FF_CHUNK, D_MODEL)),
                  _const_spec((1, D_MODEL))],
        out_specs=row,
        out_shape=jax.ShapeDtypeStruct((t, D_MODEL), F32),
        scratch_shapes=[pltpu.VMEM((tm, D_MODEL), F32)],
        compiler_params=_params("arbitrary"),
        name="ffn_final" if final else "ffn",
    )(x, norm_g, wg, wu, wd, final_g)


IN_ARR_COLS = Q_RANK + KV_RANK + 2 * LANES + RWKV_COLS
OFF_KR_ARR = Q_RANK + KV_RANK
OFF_RWKV_ARR = OFF_KR_ARR + 2 * LANES


def _inproj_body(h_ref, g_ref, win_ref, qn_ref, wuq_ref, kvn_ref, wkv_ref, cos_ref, sin_ref,
                 q_ref, k_ref, v_ref, zr_ref):
    u = _rms(h_ref[...], g_ref[...]).astype(BF16)
    z = _dot(u, win_ref[...])
    zr_ref[...] = z[:, OFF_RWKV_ARR:]

    cos = cos_ref[...]
    sin = sin_ref[...]
    cos_h = jnp.concatenate([cos] * ATT_HEADS, axis=1)
    sin_h = jnp.concatenate([sin] * ATT_HEADS, axis=1)

    q2 = _dot(_rms(z[:, :Q_RANK], qn_ref[...]).astype(BF16), wuq_ref[...])
    q = (q2[:, :QK_WIDTH] * cos_h + q2[:, QK_WIDTH:] * sin_h) * (QK_DIM ** -0.5 * math.log2(math.e))
    q_ref[...] = q.astype(BF16)

    kv = _dot(_rms(z[:, Q_RANK:OFF_KR_ARR], kvn_ref[...]).astype(BF16), wkv_ref[...])
    k_pe = z[:, OFF_KR_ARR:OFF_KR_ARR + LANES] * cos + z[:, OFF_KR_ARR + LANES:OFF_RWKV_ARR] * sin
    k = kv[:, :QK_WIDTH] + jnp.concatenate([k_pe] * ATT_HEADS, axis=1)
    k_ref[...] = k.astype(BF16)
    v = kv[:, QK_WIDTH:]
    col = lax.broadcasted_iota(jnp.int32, v.shape, 1) % (2 * HEAD_PAD)
    is_ones = (col == ONES_LANE[0]) | (col == HEAD_PAD + ONES_LANE[1])
    v_ref[...] = jnp.where(is_ones, 1.0, v).astype(BF16)


def _inproj(h, seq, mix_g, w_in_arr, q_norm, w_uq2, kv_norm, w_kv2, cos_t, sin_t):
    t = h.shape[0]
    tm = min(ROW_TILE, seq)
    per_seq = seq // tm
    row = lambda w: pl.BlockSpec((tm, w), lambda i: (i, 0))
    pos = pl.BlockSpec((tm, LANES), lambda i: (i % per_seq, 0))
    return pl.pallas_call(
        _inproj_body,
        grid=(t // tm,),
        in_specs=[row(D_MODEL), _const_spec((1, D_MODEL)), _const_spec((D_MODEL, IN_ARR_COLS)),
                  _const_spec((1, Q_RANK)), _const_spec((Q_RANK, 2 * QK_WIDTH)),
                  _const_spec((1, KV_RANK)), _const_spec((KV_RANK, 2 * QK_WIDTH)),
                  pos, pos],
        out_specs=[row(QK_WIDTH), row(QK_WIDTH), row(QK_WIDTH), row(RWKV_COLS)],
        out_shape=[jax.ShapeDtypeStruct((t, QK_WIDTH), BF16),
                   jax.ShapeDtypeStruct((t, QK_WIDTH), BF16),
                   jax.ShapeDtypeStruct((t, QK_WIDTH), BF16),
                   jax.ShapeDtypeStruct((t, RWKV_COLS), F32)],
        compiler_params=_params("arbitrary"),
        name="inproj",
    )(h, mix_g, w_in_arr, q_norm, w_uq2, kv_norm, w_kv2, cos_t, sin_t)


def _attn_body(q_ref, k_ref, v_ref, o_ref, *, tk, nk):
    tq = q_ref.shape[1]
    head_lanes = [slice(hh * HEAD_PAD, (hh + 1) * HEAD_PAD) for hh in range(2)]
    qs = [q_ref[0, :, lanes] for lanes in head_lanes]

    def step(j, carry):
        rows = pl.ds(pl.multiple_of(j * tk, tk), tk)
        new = []
        scores = [lax.dot_general(qs[hh], k_ref[0, rows, head_lanes[hh]], (((1,), (1,)), ((), ())),
                                  preferred_element_type=F32) for hh in range(2)]
        for hh in range(2):
            m, acc = carry[hh]
            vs = v_ref[0, rows, head_lanes[hh]]
            s = scores[hh]
            m_new = jnp.maximum(m, jnp.max(s, axis=-1, keepdims=True))
            p = jnp.exp2(s - m_new)
            acc = jnp.exp2(m - m_new) * acc + _dot(p.astype(BF16), vs)
            new.append((m_new, acc))
        return tuple(new)

    init = (jnp.full((tq, 1), -jnp.inf, F32), jnp.zeros((tq, LANES), F32))
    res = lax.fori_loop(0, nk, step, (init, init))
    lane = lax.broadcasted_iota(jnp.int32, (tq, LANES), 1)
    out = jnp.zeros((tq, LANES), F32)
    for hh in range(2):
        acc = res[hh][1]
        ones_lane = ONES_LANE[hh]
        is_v = (lane < V_DIM) if hh == 0 else (lane >= V_DIM)
        out = out + jnp.where(is_v, acc / acc[:, ones_lane:ones_lane + 1], 0.0)
    o_ref[0] = out


def _attention(q, k, v):
    b, s, _ = q.shape
    tq = min(ATT_TQ, s)
    tk = min(ATT_TK, s)
    pairs = ATT_HEADS // 2
    return pl.pallas_call(
        functools.partial(_attn_body, tk=tk, nk=s // tk),
        grid=(b, pairs, s // tq),
        in_specs=[pl.BlockSpec((1, tq, 2 * HEAD_PAD), lambda bi, hp, qi: (bi, qi, hp)),
                  pl.BlockSpec((1, s, 2 * HEAD_PAD), lambda bi, hp, qi: (bi, 0, hp),
                               pipeline_mode=pl.Buffered(1)),
                  pl.BlockSpec((1, s, 2 * HEAD_PAD), lambda bi, hp, qi: (bi, 0, hp),
                               pipeline_mode=pl.Buffered(1))],
        out_specs=pl.BlockSpec((1, tq, LANES), lambda bi, hp, qi: (bi, qi, hp)),
        out_shape=jax.ShapeDtypeStruct((b, s, ATT_WIDTH), F32),
        compiler_params=_params("arbitrary", "arbitrary", "arbitrary"),
        name="attention",
    )(q, k, v)


EXP_NEG_HALF = math.exp(-0.5)


def _rwkv_pre_body(z_ref, zp_ref, zn_ref, mu_ref, w0_ref, wup_ref, a0_ref, aup_ref, gup_ref,
                   kk_ref, ka_ref, rk_ref, bd_ref,
                   r_out, v_out, kk_out, wf_out, wb_out, kdf_out, kdb_out, bf_out, bb_out, g_out, bonus_out):
    i = pl.program_id(1)
    n = pl.num_programs(1)
    z = z_ref[0]
    ts = z.shape[0]
    prev_row = jnp.where(i > 0, zp_ref[0, SUBLANES - 1:SUBLANES, :], 0.0)
    next_row = jnp.where(i < n - 1, zn_ref[0, 0:1, :], 0.0)
    rows = lax.broadcasted_iota(jnp.int32, z.shape, 0)
    z_prev = jnp.where(rows == 0, prev_row, pltpu.roll(z, 1, 0))
    z_next = jnp.where(rows == ts - 1, next_row, pltpu.roll(z, ts - 1, 0))
    zs = z + mu_ref[0:1, :] * (z_prev - z) + mu_ref[1:2, :] * (z_next - z)

    w = RWKV_WIDTH
    r = zs[:, 0:w]
    k = zs[:, w:2 * w]
    v = zs[:, 2 * w:3 * w]
    wd = zs[:, 3 * w:3 * w + LANES]
    ad = zs[:, 3 * w + LANES:3 * w + 2 * LANES]
    gd = zs[:, 3 * w + 2 * LANES:]

    lora_w = _dot(jnp.tanh(wd).astype(BF16), wup_ref[...])
    lora_a = _dot(ad.astype(BF16), aup_ref[...])
    g_out[0] = _dot(jax.nn.sigmoid(gd).astype(BF16), gup_ref[...])

    ones_bd = bd_ref[...]
    kkf = k * kk_ref[...]
    kk = kkf / jnp.maximum(jnp.sqrt(_segsum(kkf * kkf, ones_bd)), 1e-12)
    r_out[0] = r
    v_out[0] = v
    kk_out[0] = kk

    k_sum = jnp.zeros_like(k)
    for d, (w_out, kd_out, b_out) in enumerate(((wf_out, kdf_out, bf_out), (wb_out, kdb_out, bb_out))):
        x = w0_ref[d:d + 1, :] + lora_w[:, d * w:(d + 1) * w]
        w_out[0] = jnp.exp(-EXP_NEG_HALF * jax.nn.sigmoid(x))
        a = jax.nn.sigmoid(a0_ref[d:d + 1, :] + lora_a[:, d * w:(d + 1) * w])
        k_d = k * (1.0 + (a - 1.0) * ka_ref[...])
        kd_out[0] = k_d
        b_out[0] = kk * a
        k_sum = k_sum + k_d
    bonus_out[0] = _segsum(r * k_sum * rk_ref[...], ones_bd) * v


def _rwkv_pre(zr, shift_mu, w0, wup_bd, a0, aup_bd, g_up, k_k, k_a, r_k, ones_bd):
    b, s, _ = zr.shape
    ts = min(PRE_TILE, s)
    halo = ts // SUBLANES
    last = s // SUBLANES - 1
    wide = lambda: pl.BlockSpec((1, ts, RWKV_WIDTH), lambda bi, i: (bi, i, 0))
    out = jax.ShapeDtypeStruct((b, s, RWKV_WIDTH), F32)
    return pl.pallas_call(
        _rwkv_pre_body,
        grid=(b, s // ts),
        in_specs=[pl.BlockSpec((1, ts, RWKV_COLS), lambda bi, i: (bi, i, 0)),
                  pl.BlockSpec((1, SUBLANES, RWKV_COLS), lambda bi, i: (bi, jnp.maximum(i * halo - 1, 0), 0)),
                  pl.BlockSpec((1, SUBLANES, RWKV_COLS), lambda bi, i: (bi, jnp.minimum((i + 1) * halo, last), 0)),
                  _const_spec((2, RWKV_COLS)),
                  _const_spec((2, RWKV_WIDTH)), _const_spec((LANES, 2 * RWKV_WIDTH)),
                  _const_spec((2, RWKV_WIDTH)), _const_spec((LANES, 2 * RWKV_WIDTH)),
                  _const_spec((GATE_LORA, RWKV_WIDTH)),
                  _const_spec((1, RWKV_WIDTH)), _const_spec((1, RWKV_WIDTH)), _const_spec((1, RWKV_WIDTH)),
                  _const_spec((RWKV_WIDTH, RWKV_WIDTH))],
        out_specs=[wide() for _ in range(11)],
        out_shape=[out] * 11,
        compiler_params=_params("arbitrary", "arbitrary"),
        name="rwkv_pre",
    )(zr, zr, zr, shift_mu, w0, wup_bd, a0, aup_bd, g_up, k_k, k_a, r_k, ones_bd)


HEAD_PAIRS = RWKV_HEADS // 2
N_CHAINS = 2 * HEAD_PAIRS
STATE_VREGS = RWKV_HEAD_DIM // SUBLANES


def _scan_body(*refs, tb, long_stream):
    n_streams = len(long_stream)
    ins = refs[:12 * n_streams]
    w2_ref, wo_ref = refs[12 * n_streams:12 * n_streams + 2]
    outs = refs[12 * n_streams + 2:14 * n_streams + 2]
    state_ref = refs[-1]

    for si, is_long in enumerate(long_stream):
        first = pl.program_id(1) == 0
        if is_long:
            first = first & (pl.program_id(0) == 0)

        @pl.when(first)
        def _(si=si):
            state_ref[si * N_CHAINS:(si + 1) * N_CHAINS] = jnp.zeros((N_CHAINS,) + state_ref.shape[1:], F32)

    w2 = w2_ref[...]
    wo = wo_ref[...]
    shape3 = (STATE_VREGS, SUBLANES, LANES)
    vrow = lax.broadcasted_iota(jnp.int32, shape3, 0) * SUBLANES + lax.broadcasted_iota(jnp.int32, shape3, 1)
    diag = vrow == lax.broadcasted_iota(jnp.int32, shape3, 2) % RWKV_HEAD_DIM

    n = RWKV_HEAD_DIM

    def group_sums(xs):
        parts = []
        for x3 in xs:
            hi, mid = _split_bf16(x3.reshape(n, LANES))
            parts.append(jnp.concatenate([hi, mid], axis=1))
        res = _dot(jnp.concatenate(parts, axis=0), w2)
        return [res[i * n:(i + 1) * n].reshape(shape3) for i in range(len(xs))]

    def paired_sums(xs):
        xb = [x3.reshape(n, LANES).astype(BF16) for x3 in xs]
        lhs = jnp.concatenate([jnp.concatenate(xb[i:i + 2], axis=1) for i in range(0, len(xb), 2)], axis=0)
        res = _dot(lhs, wo)
        return [res[(i // 2) * n:(i // 2 + 1) * n, (i % 2) * LANES:(i % 2 + 1) * LANES].reshape(shape3)
                for i in range(len(xs))]

    in_refs = [[ins[12 * si + 6 * d:12 * si + 6 * d + 6] for d in range(2)] for si in range(n_streams)]
    out_refs = [outs[2 * si:2 * si + 2] for si in range(n_streams)]

    n_groups = tb // SUBLANES
    sub = lax.broadcasted_iota(jnp.int32, (SUBLANES, LANES), 0)
    lane_k = lax.broadcasted_iota(jnp.int32, (SUBLANES, LANES), 1) % RWKV_HEAD_DIM
    lane_vreg = lane_k // SUBLANES
    lane_sub = lane_k % SUBLANES

    chains = [(si, d, hp) for si in range(n_streams) for d in range(2) for hp in range(HEAD_PAIRS)]
    n_chains = len(chains)
    groups = [range(si * N_CHAINS, (si + 1) * N_CHAINS) for si in range(n_streams)]

    def token_group(gi, carry):
        rows8 = [pl.ds(pl.multiple_of(g * SUBLANES, SUBLANES), SUBLANES) for g in (gi, n_groups - 1 - gi)]
        tiles = [[ref[0, rows8[d], hp * LANES:(hp + 1) * LANES] for ref in in_refs[si][d]] for si, d, hp in chains]
        states = [state_ref[c] for c in range(n_chains)]
        y_tiles = [jnp.zeros((SUBLANES, LANES), F32)] * n_chains
        for jj in range(SUBLANES):
            rows = []
            for c, (si, d, hp) in enumerate(chains):
                j = jj if d == 0 else SUBLANES - 1 - jj
                rows.append([t[j:j + 1, :][None] for t in tiles[c]])
            sa, v_col = [], []
            for cs in groups:
                sa += paired_sums([states[c] * rows[c][0] for c in cs])
            for cs in groups:
                v_col += paired_sums([jnp.where(diag, rows[c][5], 0.0) for c in cs])
            for c in range(n_chains):
                _, w, b, kd, _, _ = rows[c]
                states[c] = states[c] * w - sa[c] * b + v_col[c] * kd
            for cs in groups:
                o = paired_sums([states[c] * rows[c][4] for c in cs])
                for i, c in enumerate(cs):
                    d = chains[c][1]
                    j = jj if d == 0 else SUBLANES - 1 - jj
                    pick = o[i][0]
                    for vi in range(1, STATE_VREGS):
                        pick = jnp.where(lane_vreg == vi, o[i][vi], pick)
                    y = jnp.sum(jnp.where(lane_sub == sub, pick, 0.0), axis=0, keepdims=True)
                    y_tiles[c] = jnp.where(sub == j, y, y_tiles[c])
        for c, (si, d, hp) in enumerate(chains):
            state_ref[c] = states[c]
            out_refs[si][d][0, rows8[d], hp * LANES:(hp + 1) * LANES] = y_tiles[c]
        return carry

    lax.fori_loop(0, n_groups, token_group, 0)


def _rwkv_scan(streams, segments, w2, wo):
    seg_len = min(a[0].shape[1] for a, _ in streams)
    tb = min(SCAN_TB, seg_len)
    nb = seg_len // tb
    total = segments * nb
    in_specs, operands, out_specs, out_shapes, long_stream = [], [], [], [], []
    for arrays, off in streams:
        r, v, kk, w_f, w_b, kd_f, kd_b, b_f, b_b = arrays
        if off is None:
            assert r.shape[0] == 1 and r.shape[1] == segments * seg_len
            fwd_map = lambda h, j: (0, h * nb + j, 0)
            bwd_map = lambda h, j: (0, total - 1 - (h * nb + j), 0)
            out_shape = r.shape
        else:
            assert r.shape[1] == seg_len and off + segments <= r.shape[0]
            fwd_map = lambda h, j, off=off: (off + h, j, 0)
            bwd_map = lambda h, j, off=off: (off + h, nb - 1 - j, 0)
            out_shape = (segments, seg_len, RWKV_WIDTH)
        fwd = pl.BlockSpec((1, tb, RWKV_WIDTH), fwd_map)
        bwd = pl.BlockSpec((1, tb, RWKV_WIDTH), bwd_map)
        in_specs += [fwd] * 6 + [bwd] * 6
        operands += [kk, w_f, b_f, kd_f, r, v, kk, w_b, b_b, kd_b, r, v]
        out_fwd = fwd if off is None else pl.BlockSpec((1, tb, RWKV_WIDTH), lambda h, j: (h, j, 0))
        out_bwd = bwd if off is None else pl.BlockSpec((1, tb, RWKV_WIDTH), lambda h, j: (h, nb - 1 - j, 0))
        out_specs += [out_fwd, out_bwd]
        out_shapes += [jax.ShapeDtypeStruct(out_shape, F32)] * 2
        long_stream.append(off is None)
    n_streams = len(streams)
    return pl.pallas_call(
        functools.partial(_scan_body, tb=tb, long_stream=tuple(long_stream)),
        grid=(segments, nb),
        in_specs=in_specs + [_const_spec((2 * LANES, LANES)), _const_spec((2 * LANES, 2 * LANES))],
        out_specs=out_specs,
        out_shape=out_shapes,
        scratch_shapes=[pltpu.VMEM((n_streams * N_CHAINS, STATE_VREGS, SUBLANES, LANES), F32)],
        compiler_params=_params("arbitrary", "arbitrary"),
        name="rwkv_scan",
    )(*operands, w2, wo)


def _mix_out_body(att_ref, yf_ref, yb_ref, bonus_ref, g_ref, h_ref, an_ref, lnw_ref, lnb_ref, bd_ref,
                  wo_att_ref, wo_tm_ref, o_ref):
    att = _rms(att_ref[...], an_ref[...])
    ones_bd = bd_ref[...]
    y = yf_ref[...] + yb_ref[...]
    inv_n = 1.0 / RWKV_HEAD_DIM
    c = y - _segsum(y, ones_bd) * inv_n
    var = _segsum(c * c, ones_bd) * inv_n
    yn = c * lax.rsqrt(var + LN_X_EPS) * lnw_ref[...] + lnb_ref[...]
    tm = (yn + bonus_ref[...]) * g_ref[...]
    o_ref[...] = (h_ref[...] + _dot(att.astype(BF16), wo_att_ref[...])
                  + _dot(tm.astype(BF16), wo_tm_ref[...]))


def _mix_out(att, yf, yb, bonus, g, h, attn_norm, ln_w, ln_b, ones_bd, wo_att, wo_tm):
    t = h.shape[0]
    tm = min(ROW_TILE, t)
    half = pl.BlockSpec((tm, RWKV_WIDTH), lambda i: (i, 0))
    full = pl.BlockSpec((tm, D_MODEL), lambda i: (i, 0))
    vec = _const_spec((1, RWKV_WIDTH))
    return pl.pallas_call(
        _mix_out_body,
        grid=(t // tm,),
        in_specs=[half] * 5 + [full, vec, vec, vec, _const_spec((RWKV_WIDTH, RWKV_WIDTH)),
                               _const_spec((ATT_WIDTH, D_MODEL)), _const_spec((RWKV_WIDTH, D_MODEL))],
        out_specs=full,
        out_shape=jax.ShapeDtypeStruct((t, D_MODEL), F32),
        compiler_params=_params("arbitrary"),
        name="mix_out",
    )(att, yf, yb, bonus, g, h, attn_norm, ln_w, ln_b, ones_bd, wo_att, wo_tm)


def _rope_tables(seq):
    inv = 1.0 / (ROPE_THETA ** (jnp.arange(0, ROPE_DIM, 2, dtype=F32) / ROPE_DIM))
    ang = jnp.arange(seq, dtype=F32)[:, None] * inv[None, :]
    cos, sin = jnp.cos(ang), jnp.sin(ang)
    pad = jnp.zeros((seq, HEAD_PAD - QK_DIM), F32)
    cos_t = jnp.concatenate([jnp.ones((seq, NOPE_DIM), F32), cos, cos, pad], axis=1)
    sin_t = jnp.concatenate([jnp.zeros((seq, NOPE_DIM), F32), sin, sin, pad], axis=1)
    return cos_t, sin_t


def _rotate_half_cols(w):
    half = ROPE_DIM // 2
    return jnp.concatenate([-w[..., half:], w[..., :half]], axis=-1)


def _pad_head_cols(nope, rope):
    rows = nope.shape[0]
    pad = jnp.zeros((rows, ATT_HEADS, HEAD_PAD - QK_DIM), nope.dtype)
    return jnp.concatenate([nope, rope, pad], axis=-1).reshape(rows, QK_WIDTH)


def _block_diag2(a, b):
    za = jnp.zeros_like(a)
    return jnp.concatenate([jnp.concatenate([a, za], axis=1), jnp.concatenate([za, b], axis=1)], axis=0)


def _prepare(p):
    row = lambda a: a.reshape(1, -1)
    ffn_w = lambda wg, wu, wd: (
        wg.reshape(D_MODEL, N_FF_CHUNKS, FF_CHUNK).transpose(1, 0, 2).astype(BF16),
        wu.reshape(D_MODEL, N_FF_CHUNKS, FF_CHUNK).transpose(1, 0, 2).astype(BF16),
        wd.reshape(N_FF_CHUNKS, FF_CHUNK, D_MODEL).astype(BF16))

    w_in = p['w_in']
    off_ckv = Q_RANK
    off_kr = off_ckv + KV_RANK
    off_rwkv = off_kr + ROPE_DIM
    w_kr = w_in[:, off_kr:off_rwkv]
    kr_pad = lambda w: jnp.concatenate(
        [jnp.zeros((D_MODEL, NOPE_DIM), F32), w, jnp.zeros((D_MODEL, HEAD_PAD - QK_DIM), F32)], axis=1)
    w_in_arr = jnp.concatenate(
        [w_in[:, :off_kr], kr_pad(w_kr), kr_pad(_rotate_half_cols(w_kr)), w_in[:, off_rwkv:]], axis=1).astype(BF16)

    w_uq = p['w_uq'].reshape(Q_RANK, ATT_HEADS, QK_DIM)
    q_nope, q_rope = w_uq[..., :NOPE_DIM], w_uq[..., NOPE_DIM:]
    w_uq2 = jnp.concatenate(
        [_pad_head_cols(q_nope, q_rope), _pad_head_cols(jnp.zeros_like(q_nope), _rotate_half_cols(q_rope))],
        axis=1).astype(BF16)

    w_ukv = p['w_ukv'].reshape(KV_RANK, ATT_HEADS, NOPE_DIM + V_DIM)
    k_nope, v_w = w_ukv[..., :NOPE_DIM], w_ukv[..., NOPE_DIM:]
    w_k = _pad_head_cols(k_nope, jnp.zeros((KV_RANK, ATT_HEADS, ROPE_DIM), F32))
    zero_v = jnp.zeros((KV_RANK, ATT_HEADS // 2, V_DIM), F32)
    w_v = jnp.stack([jnp.concatenate([v_w[:, 0::2], zero_v], axis=-1),
                     jnp.concatenate([zero_v, v_w[:, 1::2]], axis=-1)], axis=2).reshape(KV_RANK, QK_WIDTH)
    w_kv2 = jnp.concatenate([w_k, w_v], axis=1).astype(BF16)

    head = jnp.arange(RWKV_WIDTH) // RWKV_HEAD_DIM
    ones_bd = (head[:, None] == head[None, :]).astype(BF16)
    pair = jnp.arange(LANES) // RWKV_HEAD_DIM
    ones_pair = (pair[:, None] == pair[None, :]).astype(BF16)

    return dict(
        ffn1=(row(p['ffn1_norm']),) + ffn_w(p['ffn1_w_gate'], p['ffn1_w_up'], p['ffn1_w_down']),
        ffn2=(row(p['ffn2_norm']),) + ffn_w(p['ffn2_w_gate'], p['ffn2_w_up'], p['ffn2_w_down']),
        inproj=(row(p['mix_norm']), w_in_arr, row(p['q_norm']), w_uq2, row(p['kv_norm']), w_kv2),
        pre=(p['shift_mu'], p['w0'], _block_diag2(p['w_up'][0], p['w_up'][1]).astype(BF16),
             p['a0'], _block_diag2(p['a_up'][0], p['a_up'][1]).astype(BF16), p['g_up'].astype(BF16),
             row(p['k_k']), row(p['k_a']), row(p['r_k']), ones_bd),
        scan=(jnp.concatenate([ones_pair, ones_pair], axis=0), _block_diag2(ones_pair, ones_pair)),
        out=(row(p['attn_out_norm']), row(p['ln_x_w']), row(p['ln_x_b']), ones_bd,
             p['w_out'][:ATT_WIDTH].astype(BF16), p['w_out'][ATT_WIDTH:].astype(BF16)),
    )


def _before_scan(x, w, final_g):
    b, s, _ = x.shape
    h = _ffn(x.reshape(b * s, D_MODEL), *w['ffn1'], final_g, final=False)
    cos_t, sin_t = _rope_tables(s)
    q, k, v, zr = _inproj(h, s, *w['inproj'], cos_t, sin_t)
    att = _attention(q.reshape(b, s, QK_WIDTH), k.reshape(b, s, QK_WIDTH), v.reshape(b, s, QK_WIDTH))
    *scan_in, g, bonus = _rwkv_pre(zr.reshape(b, s, RWKV_COLS), *w['pre'])
    return h, att, g, bonus, tuple(scan_in)


def _after_scan(shape, h, att, g, bonus, y_f, y_b, w, final_g):
    flat = lambda a: a.reshape(-1, RWKV_WIDTH)
    h = _mix_out(flat(att), flat(y_f), flat(y_b), flat(bonus), flat(g), h, *w['out'])
    return _ffn(h, *w['ffn2'], final_g, final=True).reshape(shape)


def _scan_all(scan_p, scan_s, w):
    bp, sp, _ = scan_p[0].shape
    bs, ss, _ = scan_s[0].shape
    segments = ss // sp
    if bs == 1 and ss == segments * sp and bp % segments == 0:
        offsets = list(range(0, bp, segments))
        outs = _rwkv_scan([(scan_p, off) for off in offsets] + [(scan_s, None)], segments, *w['scan'])
        y_p = [jnp.concatenate(outs[d:2 * len(offsets):2], axis=0) for d in range(2)]
        return y_p, outs[2 * len(offsets):]
    outs_p = _rwkv_scan([(scan_p, off) for off in range(bp)], 1, *w['scan'])
    y_p = [jnp.concatenate(outs_p[d::2], axis=0) for d in range(2)]
    outs_s = _rwkv_scan([(tuple(a[i:i + 1] for a in scan_s), None) for i in range(bs)], 1, *w['scan'])
    return y_p, [jnp.concatenate(outs_s[d::2], axis=0) for d in range(2)]


def kernel(x_prompt, x_sample, ffn1_norm, ffn1_w_gate, ffn1_w_up, ffn1_w_down, mix_norm, w_in, q_norm, w_uq,
           kv_norm, w_ukv, attn_out_norm, shift_mu, w0, w_up, a0, a_up, g_up, k_k, k_a, r_k, ln_x_w, ln_x_b,
           w_out, ffn2_norm, ffn2_w_gate, ffn2_w_up, ffn2_w_down, final_norm):
    assert ffn1_norm.shape[0] == 1, "single-layer trunk"
    p = dict(ffn1_norm=ffn1_norm[0], ffn1_w_gate=ffn1_w_gate[0], ffn1_w_up=ffn1_w_up[0],
             ffn1_w_down=ffn1_w_down[0], mix_norm=mix_norm[0], w_in=w_in[0], q_norm=q_norm[0], w_uq=w_uq[0],
             kv_norm=kv_norm[0], w_ukv=w_ukv[0], attn_out_norm=attn_out_norm[0], shift_mu=shift_mu[0],
             w0=w0[0], w_up=w_up[0], a0=a0[0], a_up=a_up[0], g_up=g_up[0], k_k=k_k[0], k_a=k_a[0],
             r_k=r_k[0], ln_x_w=ln_x_w[0], ln_x_b=ln_x_b[0], w_out=w_out[0], ffn2_norm=ffn2_norm[0],
             ffn2_w_gate=ffn2_w_gate[0], ffn2_w_up=ffn2_w_up[0], ffn2_w_down=ffn2_w_down[0])
    w = _prepare(p)
    final_g = final_norm.reshape(1, D_MODEL)
    *pre_p, scan_p = _before_scan(x_prompt, w, final_g)
    *pre_s, scan_s = _before_scan(x_sample, w, final_g)
    y_p, y_s = _scan_all(scan_p, scan_s, w)
    return (_after_scan(x_prompt.shape, *pre_p, *y_p, w, final_g),
            _after_scan(x_sample.shape, *pre_s, *y_s, w, final_g))
```
